```python
import jax, jax.numpy as jnp
from jax import lax
import numpy as np

D_MODEL = 2048
BATCH = 4
SEQ = 8192
DEPTH = 1

D_MIX = D_MODEL
D_HGRN = D_MIX // 2
HGRN_HEAD = 128
HGRN_HEADS = D_HGRN // HGRN_HEAD
HGRN_CHUNK = 64
D_ATTN = D_MIX - D_HGRN
ATTN_HEAD = 64
ATTN_HEADS = D_ATTN // ATTN_HEAD
DILATED_PATTERNS = ((128, 1), (512, 4), (2048, 16))
NORM_EPS = 1e-6
SPLIT_WIDTHS = (D_HGRN, D_HGRN, D_HGRN, D_HGRN, D_ATTN, D_ATTN, D_ATTN, D_ATTN)
D_IN = sum(SPLIT_WIDTHS)

kernel_name = "hymba_hgrn2_dilated_alibi_block"


def rms_norm(x, gain):
    xf = x.astype(jnp.float32)
    y = xf * lax.rsqrt(jnp.mean(xf * xf, axis=-1, keepdims=True) + NORM_EPS)
    return (y * gain.astype(jnp.float32)).astype(x.dtype)


def alibi_slopes(n_heads):
    return jnp.exp2(-8.0 * jnp.arange(1, n_heads + 1, dtype=jnp.float32) / n_heads)


def chunked_gated_recurrence(q, k, log_f, v):
    B, H, S, Dk = q.shape
    Dv = v.shape[-1]
    C = HGRN_CHUNK
    N = S // C
    q, k, log_f = (a.reshape(B, H, N, C, Dk) for a in (q, k, log_f))
    v = v.reshape(B, H, N, C, Dv)
    b = jnp.cumsum(log_f, axis=3)
    b_last = b[:, :, :, -1:, :]
    q_dec = q * jnp.exp(b)
    k_dec = k * jnp.exp(-b)
    k_end = k * jnp.exp(b_last - b)
    causal = jnp.tril(jnp.ones((C, C), dtype=bool))
    a_intra = jnp.where(causal, jnp.einsum('bhncd,bhnsd->bhncs', q_dec, k_dec), 0.0)
    o_intra = jnp.einsum('bhncs,bhnsv->bhncv', a_intra, v)
    kv_chunk = jnp.einsum('bhncd,bhncv->nbhdv', k_end, v)
    chunk_decay = jnp.exp(b_last[:, :, :, 0, :]).transpose(2, 0, 1, 3)

    def step(state, inp):
        dec, kv_n = inp
        return dec[..., None] * state + kv_n, state

    _, prev_states = lax.scan(step, jnp.zeros((B, H, Dk, Dv), jnp.float32),
                              (chunk_decay, kv_chunk))
    o_inter = jnp.einsum('bhncd,nbhdv->bhncv', q_dec, prev_states)
    return (o_intra + o_inter).reshape(B, H, S, Dv)


def hgrn2_mixer(q_pre, f_pre, i_pre, lb, g_norm):
    B, S, _ = q_pre.shape

    def heads(a):
        return a.astype(jnp.float32).reshape(B, S, HGRN_HEADS, HGRN_HEAD).transpose(0, 2, 1, 3)

    q = jax.nn.silu(heads(q_pre))
    lbh = lb.astype(jnp.float32).reshape(HGRN_HEADS, 1, HGRN_HEAD)
    f = lbh + (1.0 - lbh) * jax.nn.sigmoid(heads(f_pre))
    k = 1.0 - f
    o = chunked_gated_recurrence(q, k, jnp.log(f), heads(i_pre))
    o = o * lax.rsqrt(jnp.mean(o * o, axis=-1, keepdims=True) + NORM_EPS) * g_norm.astype(jnp.float32)
    return o.transpose(0, 2, 1, 3).reshape(B, S, D_HGRN)


def dilated_pattern_attention(q, k, v, slopes, window, dilation):
    B, H, S, Dh = q.shape
    d = dilation
    W = window // d
    L = S // d
    nb = -(-L // W)
    Lp = nb * W

    def regroup(a):
        a = a.reshape(B, H, L, d, Dh).transpose(0, 1, 3, 2, 4)
        a = jnp.pad(a, ((0, 0), (0, 0), (0, 0), (0, Lp - L), (0, 0)))
        return a.reshape(B, H, d, nb, W, Dh)

    def with_prev_block(a):
        prev = jnp.pad(a, ((0, 0), (0, 0), (0, 0), (1, 0), (0, 0), (0, 0)))[:, :, :, :-1]
        return jnp.concatenate([prev, a], axis=4)

    qb = regroup(q)
    kc = with_prev_block(regroup(k))
    vc = with_prev_block(regroup(v))
    s = jnp.einsum('bhrnqd,bhrnkd->bhrnqk', qb, kc) * (Dh ** -0.5)
    i_idx = jnp.arange(W)[:, None]
    j_idx = jnp.arange(2 * W)[None, :]
    delta = W + i_idx - j_idx
    blk = jnp.arange(nb)[:, None, None]
    valid = (delta >= 0) & (delta <= W) & ((blk - 1) * W + j_idx >= 0)
    bias = -slopes[:, None, None] * (d * delta).astype(jnp.float32)
    s = jnp.where(valid, s + bias[None, :, None, None], -jnp.inf)
    m = jnp.max(s, axis=-1, keepdims=True)
    p = jnp.exp(s - m)
    den = jnp.sum(p, axis=-1, keepdims=True)
    o = jnp.einsum('bhrnqk,bhrnkd->bhrnqd', p, vc) / den
    lse = (m + jnp.log(den))[..., 0]
    o = o.reshape(B, H, d, Lp, Dh)[:, :, :, :L].transpose(0, 1, 3, 2, 4).reshape(B, H, S, Dh)
    lse = lse.reshape(B, H, d, Lp)[:, :, :, :L].transpose(0, 1, 3, 2).reshape(B, H, S)
    return o, lse


def dilated_attention_mixer(q_pre, k_pre, v_pre):
    B, S, _ = q_pre.shape

    def heads(a):
        return a.astype(jnp.float32).reshape(B, S, ATTN_HEADS, ATTN_HEAD).transpose(0, 2, 1, 3)

    q, k, v = heads(q_pre), heads(k_pre), heads(v_pre)
    slopes = alibi_slopes(ATTN_HEADS)
    outs, lses = [], []
    for window, dilation in DILATED_PATTERNS:
        o, lse = dilated_pattern_attention(q, k, v, slopes, window, dilation)
        outs.append(o)
        lses.append(lse)
    weights = jax.nn.softmax(jnp.stack(lses, axis=0), axis=0)
    o = jnp.einsum('pbhs,pbhsd->bhsd', weights, jnp.stack(outs, axis=0))
    return o.transpose(0, 2, 1, 3).reshape(B, S, D_ATTN)


def setup_inputs(seed: int = 0) -> dict:
    key = jax.random.key(seed)
    ks = jax.random.split(key, 8)
    x = jax.random.normal(ks[0], (BATCH, SEQ, D_MODEL), jnp.float32)
    norm_gain = 1.0 + 0.01 * jax.random.normal(ks[1], (DEPTH, D_MODEL), jnp.float32)
    w_in = jax.random.normal(ks[2], (DEPTH, D_MODEL, D_IN), jnp.float32) * D_MODEL ** -0.5
    lb_logits = 0.1 * jax.random.normal(ks[3], (DEPTH + 1, D_HGRN), jnp.float32)
    hgrn_gnorm = 1.0 + 0.01 * jax.random.normal(ks[4], (DEPTH, HGRN_HEAD), jnp.float32)
    w_out = jax.random.normal(ks[5], (DEPTH, D_MIX, D_MODEL), jnp.float32) * D_MIX ** -0.5
    final_gain = 1.0 + 0.01 * jax.random.normal(ks[6], (D_MODEL,), jnp.float32)
    return {"x": x, "norm_gain": norm_gain, "w_in": w_in, "lb_logits": lb_logits,
            "hgrn_gnorm": hgrn_gnorm, "w_out": w_out, "final_gain": final_gain}


def reference(x, norm_gain, w_in, lb_logits, hgrn_gnorm, w_out, final_gain):
    lb_all = jnp.cumsum(jax.nn.softmax(lb_logits.astype(jnp.float32), axis=0), axis=0)
    split_points = [int(v) for v in np.cumsum(SPLIT_WIDTHS)[:-1]]
    for layer in range(DEPTH):
        h = rms_norm(x, norm_gain[layer])
        z = jnp.einsum('bsd,de->bse', h, w_in[layer])
        q_h, f_h, i_h, g_h, q_a, k_a, v_a, g_a = jnp.split(z, split_points, axis=-1)
        y_h = hgrn2_mixer(q_h, f_h, i_h, lb_all[layer], hgrn_gnorm[layer]) * jax.nn.silu(g_h.astype(jnp.float32))
        y_a = dilated_attention_mixer(q_a, k_a, v_a) * jax.nn.silu(g_a.astype(jnp.float32))
        y = jnp.concatenate([y_h, y_a], axis=-1).astype(x.dtype)
        x = x + jnp.einsum('bse,ed->bsd', y, w_out[layer])
    return rms_norm(x, final_gain)
```

```python
import functools

import jax
import jax.numpy as jnp
from jax import lax
from jax.experimental import pallas as pl
from jax.experimental.pallas import tpu as pltpu

D_MODEL = 2048
D_HGRN = 1024
HGRN_HEAD = 128
HGRN_HEADS = D_HGRN // HGRN_HEAD
HGRN_CHUNK = 64
D_ATTN = 1024
ATTN_HEAD = 64
ATTN_HEADS = D_ATTN // ATTN_HEAD
DILATED_PATTERNS = ((128, 1), (512, 4), (2048, 16))
NORM_EPS = 1e-6
D_IN = 4 * D_HGRN + 4 * D_ATTN

LANES = 128
N_COLBLK = D_IN // LANES
BAND = 128
HEAD_PAIRS = ATTN_HEADS // 2
MASK_VALUE = -1e30

CB_HQ, CB_HF, CB_HI, CB_HG = 0, 8, 16, 24
CB_AQ, CB_AK, CB_AV, CB_AG = 32, 40, 48, 56

VMEM_LIMIT = 56 * 1024 * 1024

F32 = jnp.float32
BF16 = jnp.bfloat16


def _silu(v):
    return v * jax.nn.sigmoid(v)


def _inproj_kernel(x_ref, g_ref, w_ref, o_ref, h_ref, *, row_chunk):
    tm = x_ref.shape[0]

    @pl.when(pl.program_id(1) == 0)
    def _():
        for r in range(tm // row_chunk):
            rows = pl.ds(r * row_chunk, row_chunk)
            xv = x_ref[rows, :]
            ms = jnp.mean(xv * xv, axis=-1, keepdims=True)
            h_ref[rows, :] = (xv * lax.rsqrt(ms + NORM_EPS) * g_ref[...]).astype(BF16)

    n_sub = o_ref.shape[0] // 2
    for c in range(n_sub):
        acc = jnp.dot(h_ref[...], w_ref[:, c * 2 * LANES:(c + 1) * 2 * LANES],
                      preferred_element_type=F32)
        o_ref[2 * c] = acc[:, :LANES].astype(BF16)
        o_ref[2 * c + 1] = acc[:, LANES:].astype(BF16)


def _inproj(x2, gain, w_bf16, *, tm, tn):
    m, d = x2.shape
    n = w_bf16.shape[1]
    return pl.pallas_call(
        functools.partial(_inproj_kernel, row_chunk=256),
        grid=(m // tm, n // tn),
        in_specs=[
            pl.BlockSpec((tm, d), lambda i, j: (i, 0)),
            pl.BlockSpec((1, d), lambda i, j: (0, 0)),
            pl.BlockSpec((d, tn), lambda i, j: (0, j)),
        ],
        out_specs=pl.BlockSpec((tn // LANES, tm, LANES), lambda i, j: (j, i, 0)),
        out_shape=jax.ShapeDtypeStruct((n // LANES, m, LANES), BF16),
        scratch_shapes=[pltpu.VMEM((tm, d), BF16)],
        compiler_params=pltpu.CompilerParams(
            dimension_semantics=("parallel", "arbitrary"), vmem_limit_bytes=VMEM_LIMIT),
        name="inproj",
    )(x2, gain, w_bf16)


def _hgrn_kernel(q_ref, f_ref, i_ref, g_ref, lbl_ref, gn_ref, cs_ref, o_ref, st_ref):
    ts = q_ref.shape[1]
    c_len = HGRN_CHUNK

    @pl.when(pl.program_id(2) == 0)
    def _():
        st_ref[...] = jnp.zeros_like(st_ref)

    lbl = lbl_ref[...]
    e = jnp.exp(lbl - jnp.max(lbl, axis=0, keepdims=True))
    lb = e[0:1, :] / jnp.sum(e, axis=0, keepdims=True)

    q = _silu(q_ref[0].astype(F32))
    f = lb + (1.0 - lb) * jax.nn.sigmoid(f_ref[0].astype(F32))
    k = 1.0 - f
    log_f = jnp.log(f)
    hi = log_f.astype(BF16)
    lo = (log_f - hi.astype(F32)).astype(BF16)
    cs = jnp.dot(cs_ref[...], jnp.concatenate([hi, lo], axis=-1), preferred_element_type=F32)
    b_cum = cs[:ts, :LANES] + cs[:ts, LANES:]
    b_tot = cs[ts:, :LANES] + cs[ts:, LANES:]
    q_dec = (q * jnp.exp(b_cum)).astype(BF16)
    k_dec = (k * jnp.exp(-b_cum)).astype(BF16)
    k_end = (k * jnp.exp(b_tot - b_cum)).astype(BF16)
    v = i_ref[0]
    gate = _silu(g_ref[0].astype(F32))
    gn = gn_ref[...]

    row = lax.broadcasted_iota(jnp.int32, (c_len, c_len), 0)
    col = lax.broadcasted_iota(jnp.int32, (c_len, c_len), 1)
    causal = col <= row

    nt = (((1,), (1,)), ((), ()))
    tn = (((0,), (0,)), ((), ()))
    for c in range(ts // c_len):
        sl = slice(c * c_len, (c + 1) * c_len)
        st = st_ref[...]
        a = lax.dot_general(q_dec[sl], k_dec[sl], nt, preferred_element_type=F32)
        a = jnp.where(causal, a, 0.0).astype(BF16)
        o = jnp.dot(a, v[sl], preferred_element_type=F32)
        o = o + lax.dot_general(q_dec[sl], st.astype(BF16), nt, preferred_element_type=F32)
        kv_t = lax.dot_general(v[sl], k_end[sl], tn, preferred_element_type=F32)
        decay = jnp.exp(b_tot[c * c_len:c * c_len + 1, :])
        st_ref[...] = st * decay + kv_t
        o = o * lax.rsqrt(jnp.mean(o * o, axis=-1, keepdims=True) + NORM_EPS) * gn
        o_ref[0, sl, :] = (o * gate[sl]).astype(BF16)


def _cumsum_operator(ts):
    r = jnp.arange(ts)[:, None]
    c = jnp.arange(ts)[None, :]
    same = (r // HGRN_CHUNK) == (c // HGRN_CHUNK)
    tril = same & (c <= r)
    return jnp.concatenate([tril, same], axis=0).astype(BF16)


def _hgrn(z3, lb_logits, gnorm, *, batch, seq, ts):
    nst = seq // ts

    def zspec(cb):
        return pl.BlockSpec((1, ts, LANES), lambda b, h, s: (cb + h, b * nst + s, 0))

    return pl.pallas_call(
        _hgrn_kernel,
        grid=(batch, HGRN_HEADS, nst),
        in_specs=[
            zspec(CB_HQ), zspec(CB_HF), zspec(CB_HI), zspec(CB_HG),
            pl.BlockSpec((lb_logits.shape[0], LANES), lambda b, h, s: (0, h)),
            pl.BlockSpec((1, LANES), lambda b, h, s: (0, 0)),
            pl.BlockSpec((2 * ts, ts), lambda b, h, s: (0, 0)),
        ],
        out_specs=pl.BlockSpec((1, ts, LANES), lambda b, h, s: (h, b * nst + s, 0)),
        out_shape=jax.ShapeDtypeStruct((HGRN_HEADS, batch * seq, LANES), BF16),
        scratch_shapes=[pltpu.VMEM((HGRN_HEAD, HGRN_HEAD), F32)],
        compiler_params=pltpu.CompilerParams(
            dimension_semantics=("parallel", "parallel", "arbitrary"), vmem_limit_bytes=VMEM_LIMIT),
        name="hgrn2",
    )(z3, z3, z3, z3, lb_logits, gnorm, _cumsum_operator(ts))


def _attn_kernel(slopes_ref, q_ref, k_ref, v_ref, g_ref, o_ref,
                 bias_ref, qf_ref, kf_ref, vf_ref, acc_ref, mx_ref, den_ref, *, row_chunk):
    t_len = q_ref.shape[1]
    hp = pl.program_id(1)
    tile = pl.program_id(2)

    @pl.when(tile == 0)
    def _():
        kf_ref[:t_len, :] = jnp.zeros((t_len, LANES), F32)
        vf_ref[:t_len, :] = jnp.zeros((t_len, LANES), F32)
        qi = lax.broadcasted_iota(jnp.int32, (BAND, 2 * BAND), 0)
        kj = lax.broadcasted_iota(jnp.int32, (BAND, 2 * BAND), 1)
        delta = BAND + qi - kj
        in_band = (delta >= 0) & (delta <= BAND)
        not_prev = kj >= BAND
        delta_f = delta.astype(F32)
        for p, (_, dil) in enumerate(DILATED_PATTERNS):
            for hh in range(2):
                slope = slopes_ref[2 * hp + hh]
                bias = -(slope * float(dil)) * delta_f
                bias_ref[p, hh, 0] = jnp.where(in_band, bias, MASK_VALUE)
                bias_ref[p, hh, 1] = jnp.where(in_band & not_prev, bias, MASK_VALUE)

    @pl.when(tile > 0)
    def _():
        kf_ref[:t_len, :] = kf_ref[t_len:, :]
        vf_ref[:t_len, :] = vf_ref[t_len:, :]

    kf_ref[t_len:, :] = k_ref[0].astype(F32)
    vf_ref[t_len:, :] = v_ref[0].astype(F32)
    qf_ref[...] = q_ref[0].astype(F32) * (ATTN_HEAD ** -0.5)

    lane = lax.broadcasted_iota(jnp.int32, (1, LANES), 1)
    head0 = lane < ATTN_HEAD
    nt = (((1,), (1,)), ((), ()))

    for p, (_, dil) in enumerate(DILATED_PATTERNS):
        blk_rows = BAND * dil
        nblk = t_len // blk_rows
        shift = nblk.bit_length() - 1

        def unit(u, carry, p=p, dil=dil, blk_rows=blk_rows, nblk=nblk, shift=shift):
            r = lax.shift_right_logical(u, shift)
            n = lax.bitwise_and(u, nblk - 1)
            q0 = n * blk_rows + r
            k0 = t_len + q0 - blk_rows
            if dil == 1:
                q_idx = pl.ds(q0, BAND)
                k_idx = pl.ds(k0, 2 * BAND)
            else:
                q_idx = pl.ds(q0, BAND, stride=dil)
                k_idx = pl.ds(k0, 2 * BAND, stride=dil)
            qb = qf_ref[q_idx, :]
            kb = kf_ref[k_idx, :].astype(BF16)
            vb = vf_ref[k_idx, :]
            q2 = jnp.concatenate([jnp.where(head0, qb, 0.0), jnp.where(head0, 0.0, qb)], axis=0)
            s = lax.dot_general(q2.astype(BF16), kb, nt, preferred_element_type=F32)
            first = jnp.where((tile == 0) & (n == 0), 1, 0)
            s0 = s[:BAND] + bias_ref[p, 0, first]
            s1 = s[BAND:] + bias_ref[p, 1, first]
            m0 = jnp.max(s0, axis=-1, keepdims=True)
            m1 = jnp.max(s1, axis=-1, keepdims=True)
            p0 = jnp.exp(s0 - m0)
            p1 = jnp.exp(s1 - m1)
            d0 = jnp.sum(p0, axis=-1, keepdims=True)
            d1 = jnp.sum(p1, axis=-1, keepdims=True)
            pc = jnp.concatenate([p0, p1], axis=1).astype(BF16)
            v2 = jnp.concatenate([jnp.where(head0, vb, 0.0), jnp.where(head0, 0.0, vb)], axis=0)
            acc_ref[p, q_idx, :] = jnp.dot(pc, v2.astype(BF16), preferred_element_type=F32)
            mx_ref[p, q_idx, :] = jnp.where(head0, m0, m1)
            den_ref[p, q_idx, :] = jnp.where(head0, d0, d1)
            return carry

        lax.fori_loop(0, t_len // BAND, unit, 0)

    def combine(c, carry):
        rows = pl.ds(pl.multiple_of(c * row_chunk, row_chunk), row_chunk)
        m_all = jnp.maximum(jnp.maximum(mx_ref[0, rows, :], mx_ref[1, rows, :]), mx_ref[2, rows, :])
        num = jnp.zeros((row_chunk, LANES), F32)
        den = jnp.zeros((row_chunk, LANES), F32)
        for p in range(len(DILATED_PATTERNS)):
            w = jnp.exp(mx_ref[p, rows, :] - m_all)
            num = num + w * acc_ref[p, rows, :]
            den = den + w * den_ref[p, rows, :]
        gate = _silu(g_ref[0, rows, :].astype(F32))
        o_ref[0, rows, :] = (num / den * gate).astype(BF16)
        return carry

    lax.fori_loop(0, t_len // row_chunk, combine, 0)


def _attn(z3, slopes, *, batch, seq, t_len):
    nt_ = seq // t_len
    n_pat = len(DILATED_PATTERNS)

    def zspec(cb):
        return pl.BlockSpec((1, t_len, LANES), lambda b, h, t, sl: (cb + h, b * nt_ + t, 0))

    grid_spec = pltpu.PrefetchScalarGridSpec(
        num_scalar_prefetch=1,
        grid=(batch, HEAD_PAIRS, nt_),
        in_specs=[zspec(CB_AQ), zspec(CB_AK), zspec(CB_AV), zspec(CB_AG)],
        out_specs=pl.BlockSpec((1, t_len, LANES), lambda b, h, t, sl: (h, b * nt_ + t, 0)),
        scratch_shapes=[
            pltpu.VMEM((n_pat, 2, 2, BAND, 2 * BAND), F32),
            pltpu.VMEM((t_len, LANES), F32),
            pltpu.VMEM((2 * t_len, LANES), F32),
            pltpu.VMEM((2 * t_len, LANES), F32),
            pltpu.VMEM((n_pat, t_len, LANES), F32),
            pltpu.VMEM((n_pat, t_len, LANES), F32),
            pltpu.VMEM((n_pat, t_len, LANES), F32),
        ],
    )
    return pl.pallas_call(
        functools.partial(_attn_kernel, row_chunk=256),
        grid_spec=grid_spec,
        out_shape=jax.ShapeDtypeStruct((HEAD_PAIRS, batch * seq, LANES), BF16),
        compiler_params=pltpu.CompilerParams(
            dimension_semantics=("parallel", "parallel", "arbitrary"), vmem_limit_bytes=VMEM_LIMIT),
        name="dilated_attn",
    )(slopes, z3, z3, z3, z3)


def _outproj_kernel(yh_ref, ya_ref, x_ref, w_ref, g_ref, o_ref):
    y = jnp.concatenate([yh_ref[c] for c in range(yh_ref.shape[0])]
                        + [ya_ref[c] for c in range(ya_ref.shape[0])], axis=-1)
    r = x_ref[...] + jnp.dot(y, w_ref[...], preferred_element_type=F32)
    ms = jnp.mean(r * r, axis=-1, keepdims=True)
    o_ref[...] = r * lax.rsqrt(ms + NORM_EPS) * g_ref[...]


def _outproj(yh3, ya3, x2, w_bf16, gain, *, tm):
    m, d = x2.shape
    return pl.pallas_call(
        _outproj_kernel,
        grid=(m // tm,),
        in_specs=[
            pl.BlockSpec((yh3.shape[0], tm, LANES), lambda i: (0, i, 0)),
            pl.BlockSpec((ya3.shape[0], tm, LANES), lambda i: (0, i, 0)),
            pl.BlockSpec((tm, d), lambda i: (i, 0)),
            pl.BlockSpec(w_bf16.shape, lambda i: (0, 0)),
            pl.BlockSpec((1, d), lambda i: (0, 0)),
        ],
        out_specs=pl.BlockSpec((tm, d), lambda i: (i, 0)),
        out_shape=jax.ShapeDtypeStruct((m, d), F32),
        compiler_params=pltpu.CompilerParams(
            dimension_semantics=("parallel",), vmem_limit_bytes=VMEM_LIMIT),
        name="outproj",
    )(yh3, ya3, x2, w_bf16, gain)


def kernel(x, norm_gain, w_in, lb_logits, hgrn_gnorm, w_out, final_gain):
    batch, seq, d = x.shape
    assert d == D_MODEL and w_in.shape == (1, D_MODEL, D_IN) and lb_logits.shape[0] == 2
    m = batch * seq
    t_len = BAND * DILATED_PATTERNS[-1][1]
    assert seq % t_len == 0 and m % 1024 == 0

    x2 = x.reshape(m, d)
    z3 = _inproj(x2, norm_gain[0:1], w_in[0].astype(BF16), tm=1024, tn=1024)
    yh3 = _hgrn(z3, lb_logits, hgrn_gnorm[0:1], batch=batch, seq=seq, ts=512)
    slopes = jnp.exp2(-8.0 * jnp.arange(1, ATTN_HEADS + 1, dtype=F32) / ATTN_HEADS)
    ya3 = _attn(z3, slopes, batch=batch, seq=seq, t_len=t_len)
    out = _outproj(yh3, ya3, x2, w_out[0].astype(BF16), final_gain.reshape(1, d), tm=512)
    return out.reshape(batch, seq, d)
```

```python
import functools

import jax
import jax.numpy as jnp
from jax import lax
from jax.experimental import pallas as pl
from jax.experimental.pallas import tpu as pltpu

D_MODEL = 2048
D_HGRN = 1024
HGRN_HEAD = 128
HGRN_HEADS = D_HGRN // HGRN_HEAD
HGRN_CHUNK = 64
D_ATTN = 1024
ATTN_HEAD = 64
ATTN_HEADS = D_ATTN // ATTN_HEAD
DILATED_PATTERNS = ((128, 1), (512, 4), (2048, 16))
NORM_EPS = 1e-6
D_IN = 4 * D_HGRN + 4 * D_ATTN

LANES = 128
N_COLBLK = D_IN // LANES
BAND = 128
HEAD_PAIRS = ATTN_HEADS // 2
MASK_VALUE = -1e30

CB_HQ, CB_HF, CB_HI, CB_HG = 0, 8, 16, 24
CB_AQ, CB_AK, CB_AV, CB_AG = 32, 40, 48, 56

VMEM_LIMIT = 56 * 1024 * 1024

F32 = jnp.float32
BF16 = jnp.bfloat16


def _silu(v):
    return v * jax.nn.sigmoid(v)


def _inproj_kernel(x_ref, g_ref, w_ref, o_ref, h_ref, *, row_chunk):
    tm = x_ref.shape[0]

    @pl.when(pl.program_id(1) == 0)
    def _():
        for r in range(tm // row_chunk):
            rows = pl.ds(r * row_chunk, row_chunk)
            xv = x_ref[rows, :]
            ms = jnp.mean(xv * xv, axis=-1, keepdims=True)
            h_ref[rows, :] = (xv * lax.rsqrt(ms + NORM_EPS) * g_ref[...]).astype(BF16)

    n_sub = o_ref.shape[0] // 2
    for c in range(n_sub):
        acc = jnp.dot(h_ref[...], w_ref[:, c * 2 * LANES:(c + 1) * 2 * LANES],
                      preferred_element_type=F32)
        o_ref[2 * c] = acc[:, :LANES].astype(BF16)
        o_ref[2 * c + 1] = acc[:, LANES:].astype(BF16)


def _inproj(x2, gain, w_bf16, *, tm, tn):
    m, d = x2.shape
    n = w_bf16.shape[1]
    return pl.pallas_call(
        functools.partial(_inproj_kernel, row_chunk=256),
        grid=(m // tm, n // tn),
        in_specs=[
            pl.BlockSpec((tm, d), lambda i, j: (i, 0)),
            pl.BlockSpec((1, d), lambda i, j: (0, 0)),
            pl.BlockSpec((d, tn), lambda i, j: (0, j)),
        ],
        out_specs=pl.BlockSpec((tn // LANES, tm, LANES), lambda i, j: (j, i, 0)),
        out_shape=jax.ShapeDtypeStruct((n // LANES, m, LANES), BF16),
        scratch_shapes=[pltpu.VMEM((tm, d), BF16)],
        compiler_params=pltpu.CompilerParams(
            dimension_semantics=("parallel", "arbitrary"), vmem_limit_bytes=VMEM_LIMIT),
        name="inproj",
    )(x2, gain, w_bf16)


def _hgrn_kernel(q_ref, f_ref, i_ref, g_ref, lbl_ref, gn_ref, o_ref, st_ref):
    ts = q_ref.shape[1]
    c_len = HGRN_CHUNK
    row = lax.broadcasted_iota(jnp.int32, (c_len, c_len), 0)
    col = lax.broadcasted_iota(jnp.int32, (c_len, c_len), 1)
    causal = col <= row
    tril = causal.astype(BF16)

    @pl.when(pl.program_id(2) == 0)
    def _():
        st_ref[...] = jnp.zeros_like(st_ref)

    lbl = lbl_ref[...]
    e = jnp.exp(lbl - jnp.max(lbl, axis=0, keepdims=True))
    lb = e[0:1, :] / jnp.sum(e, axis=0, keepdims=True)

    q = _silu(q_ref[0].astype(F32))
    f = lb + (1.0 - lb) * jax.nn.sigmoid(f_ref[0].astype(F32))
    k = 1.0 - f
    log_f = jnp.log(f)
    chunks = [slice(c * c_len, (c + 1) * c_len) for c in range(ts // c_len)]
    hi = log_f.astype(BF16)
    lo = (log_f - hi.astype(F32)).astype(BF16)
    hilo = jnp.concatenate([hi, lo], axis=-1)
    cs = jnp.concatenate(
        [jnp.dot(tril, hilo[sl], preferred_element_type=F32) for sl in chunks], axis=0)
    b_cum = cs[:, :LANES] + cs[:, LANES:]
    log_decay = [b_cum[sl.stop - 1:sl.stop, :] for sl in chunks]
    decay = [jnp.exp(ld) for ld in log_decay]
    q_dec = (q * jnp.exp(b_cum)).astype(BF16)
    k_dec_f = k * jnp.exp(-b_cum)
    k_dec = k_dec_f.astype(BF16)
    k_end = jnp.concatenate([k_dec_f[sl] * decay[c] for c, sl in enumerate(chunks)], axis=0).astype(BF16)
    v = i_ref[0]
    gate = _silu(g_ref[0].astype(F32))
    gn = gn_ref[...]

    nt = (((1,), (1,)), ((), ()))
    tn = (((0,), (0,)), ((), ()))
    kv_t = [lax.dot_general(v[sl], k_end[sl], tn, preferred_element_type=F32) for sl in chunks]
    a = [lax.dot_general(q_dec[sl], k_dec[sl], nt, preferred_element_type=F32) for sl in chunks]
    a = [jnp.where(causal, a_c, 0.0).astype(BF16) for a_c in a]
    st = st_ref[...]
    entering = []
    for c in range(len(chunks)):
        entering.append(st.astype(BF16))
        st = st * decay[c] + kv_t[c]
    st_ref[...] = st
    o = jnp.concatenate(
        [jnp.dot(a[c], v[sl], preferred_element_type=F32)
         + lax.dot_general(q_dec[sl], entering[c], nt, preferred_element_type=F32)
         for c, sl in enumerate(chunks)], axis=0)
    o = o * lax.rsqrt(jnp.mean(o * o, axis=-1, keepdims=True) + NORM_EPS) * gn
    o_ref[0] = (o * gate).astype(BF16)


def _hgrn(z3, lb_logits, gnorm, *, batch, seq, ts):
    nst = seq // ts

    def zspec(cb):
        return pl.BlockSpec((1, ts, LANES), lambda b, h, s: (cb + h, b * nst + s, 0))

    return pl.pallas_call(
        _hgrn_kernel,
        grid=(batch, HGRN_HEADS, nst),
        in_specs=[
            zspec(CB_HQ), zspec(CB_HF), zspec(CB_HI), zspec(CB_HG),
            pl.BlockSpec((lb_logits.shape[0], LANES), lambda b, h, s: (0, h)),
            pl.BlockSpec((1, LANES), lambda b, h, s: (0, 0)),
        ],
        out_specs=pl.BlockSpec((1, ts, LANES), lambda b, h, s: (h, b * nst + s, 0)),
        out_shape=jax.ShapeDtypeStruct((HGRN_HEADS, batch * seq, LANES), BF16),
        scratch_shapes=[pltpu.VMEM((HGRN_HEAD, HGRN_HEAD), F32)],
        compiler_params=pltpu.CompilerParams(
            dimension_semantics=("parallel", "parallel", "arbitrary"), vmem_limit_bytes=VMEM_LIMIT),
        name="hgrn2",
    )(z3, z3, z3, z3, lb_logits, gnorm)


def _attn_kernel(slopes_ref, q_ref, k_ref, v_ref, g_ref, o_ref,
                 bias_ref, q0_ref, q1_ref, kf_ref, v0_ref, v1_ref, acc_ref, mx_ref, den_ref,
                 *, row_chunk, unroll):
    t_len = q_ref.shape[1]
    hp = pl.program_id(1)
    tile = pl.program_id(2)
    lane = lax.broadcasted_iota(jnp.int32, (1, LANES), 1)
    head0 = lane < ATTN_HEAD

    slot = lax.bitwise_and(tile, 1)

    @pl.when(tile == 0)
    def _():
        zeros = jnp.zeros((t_len, LANES), F32)
        kf_ref[1] = zeros
        v0_ref[1] = zeros
        v1_ref[1] = zeros
        qi = lax.broadcasted_iota(jnp.int32, (BAND, 2 * BAND), 0)
        kj = lax.broadcasted_iota(jnp.int32, (BAND, 2 * BAND), 1)
        delta = BAND + qi - kj
        in_band = (delta >= 0) & (delta <= BAND)
        not_prev = kj >= BAND
        delta_f = delta.astype(F32)
        for p, (_, dil) in enumerate(DILATED_PATTERNS):
            for hh in range(2):
                slope = slopes_ref[2 * hp + hh]
                bias = -(slope * float(dil)) * delta_f
                bias_ref[p, hh, 0] = jnp.where(in_band, bias, MASK_VALUE)
                bias_ref[p, hh, 1] = jnp.where(in_band & not_prev, bias, MASK_VALUE)

    kf_ref[slot] = k_ref[0].astype(F32)
    vf = v_ref[0].astype(F32)
    v0_ref[slot] = jnp.where(head0, vf, 0.0)
    v1_ref[slot] = jnp.where(head0, 0.0, vf)
    qf = q_ref[0].astype(F32) * (ATTN_HEAD ** -0.5)
    q0_ref[...] = jnp.where(head0, qf, 0.0)
    q1_ref[...] = jnp.where(head0, 0.0, qf)

    key_row = lax.broadcasted_iota(jnp.int32, (4 * BAND, LANES), 0)
    key_lane = lax.broadcasted_iota(jnp.int32, (4 * BAND, LANES), 1)
    den_cols = ((key_row < 2 * BAND) == (key_lane < ATTN_HEAD)).astype(BF16)
    nt = (((1,), (1,)), ((), ()))

    for p, (_, dil) in enumerate(DILATED_PATTERNS):
        blk_rows = BAND * dil
        nblk = t_len // blk_rows
        shift = nblk.bit_length() - 1

        def unit(u, carry, p=p, dil=dil, blk_rows=blk_rows, nblk=nblk, shift=shift):
            r = lax.shift_right_logical(u, shift)
            n = lax.bitwise_and(u, nblk - 1)
            q_start = n * blk_rows + r
            wraps = n == 0
            p_buf = jnp.where(wraps, 1 - slot, slot)
            p_start = jnp.where(wraps, t_len - blk_rows + r, q_start - blk_rows)
            if dil == 1:
                q_idx = pl.ds(q_start, BAND)
                p_idx = pl.ds(p_start, BAND)
            else:
                q_idx = pl.ds(q_start, BAND, stride=dil)
                p_idx = pl.ds(p_start, BAND, stride=dil)

            def keys(ref):
                return jnp.concatenate([ref[p_buf, p_idx, :], ref[slot, q_idx, :]], axis=0)

            q2 = jnp.concatenate([q0_ref[q_idx, :], q1_ref[q_idx, :]], axis=0).astype(BF16)
            s = lax.dot_general(q2, keys(kf_ref).astype(BF16), nt, preferred_element_type=F32)
            first = jnp.where((tile == 0) & wraps, 1, 0)
            s0 = s[:BAND] + bias_ref[p, 0, first]
            s1 = s[BAND:] + bias_ref[p, 1, first]
            m0 = jnp.max(s0, axis=-1, keepdims=True)
            m1 = jnp.max(s1, axis=-1, keepdims=True)
            pc = jnp.concatenate([jnp.exp(s0 - m0), jnp.exp(s1 - m1)], axis=1).astype(BF16)
            v2 = jnp.concatenate([keys(v0_ref), keys(v1_ref)], axis=0).astype(BF16)
            od = jnp.dot(pc, jnp.concatenate([v2, den_cols], axis=1), preferred_element_type=F32)
            acc_ref[p, q_idx, :] = od[:, :LANES]
            den_ref[p, q_idx, :] = od[:, LANES:]
            mx_ref[p, q_idx, :] = jnp.where(head0, m0, m1)
            return carry

        lax.fori_loop(0, t_len // BAND, unit, 0, unroll=unroll)

    def combine(c, carry):
        rows = pl.ds(pl.multiple_of(c * row_chunk, row_chunk), row_chunk)
        m_all = jnp.maximum(jnp.maximum(mx_ref[0, rows, :], mx_ref[1, rows, :]), mx_ref[2, rows, :])
        num = jnp.zeros((row_chunk, LANES), F32)
        den = jnp.zeros((row_chunk, LANES), F32)
        for p in range(len(DILATED_PATTERNS)):
            w = jnp.exp(mx_ref[p, rows, :] - m_all)
            num = num + w * acc_ref[p, rows, :]
            den = den + w * den_ref[p, rows, :]
        gate = _silu(g_ref[0, rows, :].astype(F32))
        o_ref[0, rows, :] = (num / den * gate).astype(BF16)
        return carry

    lax.fori_loop(0, t_len // row_chunk, combine, 0)


def _attn(z3, slopes, *, batch, seq, t_len):
    nt_ = seq // t_len
    n_pat = len(DILATED_PATTERNS)

    def zspec(cb):
        return pl.BlockSpec((1, t_len, LANES), lambda b, h, t, sl: (cb + h, b * nt_ + t, 0))

    grid_spec = pltpu.PrefetchScalarGridSpec(
        num_scalar_prefetch=1,
        grid=(batch, HEAD_PAIRS, nt_),
        in_specs=[zspec(CB_AQ), zspec(CB_AK), zspec(CB_AV), zspec(CB_AG)],
        out_specs=pl.BlockSpec((1, t_len, LANES), lambda b, h, t, sl: (h, b * nt_ + t, 0)),
        scratch_shapes=[
            pltpu.VMEM((n_pat, 2, 2, BAND, 2 * BAND), F32),
            pltpu.VMEM((t_len, LANES), F32),
            pltpu.VMEM((t_len, LANES), F32),
            pltpu.VMEM((2, t_len, LANES), F32),
            pltpu.VMEM((2, t_len, LANES), F32),
            pltpu.VMEM((2, t_len, LANES), F32),
            pltpu.VMEM((n_pat, t_len, LANES), F32),
            pltpu.VMEM((n_pat, t_len, LANES), F32),
            pltpu.VMEM((n_pat, t_len, LANES), F32),
        ],
    )
    return pl.pallas_call(
        functools.partial(_attn_kernel, row_chunk=256, unroll=4),
        grid_spec=grid_spec,
        out_shape=jax.ShapeDtypeStruct((HEAD_PAIRS, batch * seq, LANES), BF16),
        compiler_params=pltpu.CompilerParams(
            dimension_semantics=("parallel", "parallel", "arbitrary"), vmem_limit_bytes=VMEM_LIMIT),
        name="dilated_attn",
    )(slopes, z3, z3, z3, z3)


def _outproj_kernel(yh_ref, ya_ref, x_ref, w_ref, g_ref, o_ref):
    y = jnp.concatenate([yh_ref[c] for c in range(yh_ref.shape[0])]
                        + [ya_ref[c] for c in range(ya_ref.shape[0])], axis=-1)
    r = x_ref[...] + jnp.dot(y, w_ref[...], preferred_element_type=F32)
    ms = jnp.mean(r * r, axis=-1, keepdims=True)
    o_ref[...] = r * lax.rsqrt(ms + NORM_EPS) * g_ref[...]


def _outproj(yh3, ya3, x2, w_bf16, gain, *, tm):
    m, d = x2.shape
    return pl.pallas_call(
        _outproj_kernel,
        grid=(m // tm,),
        in_specs=[
            pl.BlockSpec((yh3.shape[0], tm, LANES), lambda i: (0, i, 0)),
            pl.BlockSpec((ya3.shape[0], tm, LANES), lambda i: (0, i, 0)),
            pl.BlockSpec((tm, d), lambda i: (i, 0)),
            pl.BlockSpec(w_bf16.shape, lambda i: (0, 0)),
            pl.BlockSpec((1, d), lambda i: (0, 0)),
        ],
        out_specs=pl.BlockSpec((tm, d), lambda i: (i, 0)),
        out_shape=jax.ShapeDtypeStruct((m, d), F32),
        compiler_params=pltpu.CompilerParams(
            dimension_semantics=("parallel",), vmem_limit_bytes=VMEM_LIMIT),
        name="outproj",
    )(yh3, ya3, x2, w_bf16, gain)


def kernel(x, norm_gain, w_in, lb_logits, hgrn_gnorm, w_out, final_gain):
    batch, seq, d = x.shape
    assert d == D_MODEL and w_in.shape == (1, D_MODEL, D_IN) and lb_logits.shape[0] == 2
    m = batch * seq
    t_len = BAND * DILATED_PATTERNS[-1][1]
    assert seq % t_len == 0 and m % 1024 == 0

    x2 = x.reshape(m, d)
    z3 = _inproj(x2, norm_gain[0:1], w_in[0].astype(BF16), tm=1024, tn=1024)
    yh3 = _hgrn(z3, lb_logits, hgrn_gnorm[0:1], batch=batch, seq=seq, ts=1024)
    slopes = jnp.exp2(-8.0 * jnp.arange(1, ATTN_HEADS + 1, dtype=F32) / ATTN_HEADS)
    ya3 = _attn(z3, slopes, batch=batch, seq=seq, t_len=t_len)
    out = _outproj(yh3, ya3, x2, w_out[0].astype(BF16), final_gain.reshape(1, d), tm=512)
    return out.reshape(batch, seq, d)
```

```python
import functools

import jax
import jax.numpy as jnp
from jax import lax
from jax.experimental import pallas as pl
from jax.experimental.pallas import tpu as pltpu

D_MODEL = 2048
D_HGRN = 1024
HGRN_HEAD = 128
HGRN_HEADS = D_HGRN // HGRN_HEAD
HGRN_CHUNK = 64
D_ATTN = 1024
ATTN_HEAD = 64
ATTN_HEADS = D_ATTN // ATTN_HEAD
DILATED_PATTERNS = ((128, 1), (512, 4), (2048, 16))
NORM_EPS = 1e-6

LANES = 128
SUBLANES = 8
BAND = 128
HEAD_PAIRS = ATTN_HEADS // 2
MASK_VALUE = -1e30
RES = 16
ATTN_TILE = BAND * RES
UNIT_GROUP = 8

CB_HQ, CB_HF, CB_HI, CB_HG = 0, 8, 16, 24
CB_AQ, CB_AK, CB_AV, CB_AG = 0, 8, 16, 24

VMEM_LIMIT = 56 * 1024 * 1024

F32 = jnp.float32
BF16 = jnp.bfloat16
NT_DIMS = (((1,), (1,)), ((), ()))
TN_DIMS = (((0,), (0,)), ((), ()))


def _silu(v):
    return v * jax.nn.sigmoid(v)


def _inproj_kernel(*refs, row_chunk, by_residue):
    x_refs, (g_ref, w_ref, o_ref, h_ref) = refs[:-4], refs[-4:]
    tm, d = h_ref.shape
    per_res = tm // RES

    @pl.when(pl.program_id(1) == 0)
    def _():
        if by_residue:
            def residue(r, carry):
                src = pl.ds(r, per_res, stride=RES)
                dst = pl.ds(pl.multiple_of(r * per_res, per_res), per_res)
                cols = [x_ref[src, :] for x_ref in x_refs]
                ssq = cols[0] * cols[0]
                for xc in cols[1:]:
                    ssq = ssq + xc * xc
                scale = lax.rsqrt(jnp.sum(ssq, axis=-1, keepdims=True) * (1.0 / d) + NORM_EPS)
                for c, xc in enumerate(cols):
                    lanes = slice(c * LANES, (c + 1) * LANES)
                    h_ref[dst, lanes] = (xc * scale * g_ref[:, lanes]).astype(BF16)
                return carry

            lax.fori_loop(0, RES, residue, 0, unroll=4)
        else:
            x_ref, = x_refs
            for c in range(tm // row_chunk):
                rows = pl.ds(c * row_chunk, row_chunk)
                xv = x_ref[rows, :]
                ms = jnp.mean(xv * xv, axis=-1, keepdims=True)
                h_ref[rows, :] = (xv * lax.rsqrt(ms + NORM_EPS) * g_ref[...]).astype(BF16)

    for c in range(o_ref.shape[0] // 2):
        acc = jnp.dot(h_ref[...], w_ref[:, c * 2 * LANES:(c + 1) * 2 * LANES],
                      preferred_element_type=F32)
        for half in range(2):
            part = acc[:, half * LANES:(half + 1) * LANES].astype(BF16)
            if by_residue:
                for r in range(RES):
                    o_ref[2 * c + half, 0, r] = part[r * per_res:(r + 1) * per_res]
            else:
                o_ref[2 * c + half] = part


def _inproj(x2, gain, w_bf16, *, tm, tn, by_residue):
    m, d = x2.shape
    n = w_bf16.shape[1]
    if by_residue:
        halves = ATTN_TILE // tm
        out_shape = jax.ShapeDtypeStruct((n // LANES, m // ATTN_TILE, RES, BAND, LANES), BF16)
        out_spec = pl.BlockSpec((tn // LANES, 1, RES, tm // RES, LANES),
                                lambda i, j: (j, i // halves, 0, i % halves, 0))
        x_specs = [pl.BlockSpec((tm, LANES), lambda i, j, c=c: (i, c)) for c in range(d // LANES)]
    else:
        out_shape = jax.ShapeDtypeStruct((n // LANES, m, LANES), BF16)
        out_spec = pl.BlockSpec((tn // LANES, tm, LANES), lambda i, j: (j, i, 0))
        x_specs = [pl.BlockSpec((tm, d), lambda i, j: (i, 0))]
    return pl.pallas_call(
        functools.partial(_inproj_kernel, row_chunk=256, by_residue=by_residue),
        grid=(m // tm, n // tn),
        in_specs=x_specs + [
            pl.BlockSpec((1, d), lambda i, j: (0, 0)),
            pl.BlockSpec((d, tn), lambda i, j: (0, j)),
        ],
        out_specs=out_spec,
        out_shape=out_shape,
        scratch_shapes=[pltpu.VMEM((tm, d), BF16)],
        compiler_params=pltpu.CompilerParams(
            dimension_semantics=("parallel", "arbitrary"), vmem_limit_bytes=VMEM_LIMIT),
        name="inproj_attn" if by_residue else "inproj_hgrn",
    )(*([x2] * len(x_specs)), gain, w_bf16)


def _hgrn_kernel(q_ref, f_ref, i_ref, g_ref, lbl_ref, gn_ref, o_ref, st_ref):
    ts = q_ref.shape[1]
    c_len = HGRN_CHUNK
    row = lax.broadcasted_iota(jnp.int32, (c_len, c_len), 0)
    col = lax.broadcasted_iota(jnp.int32, (c_len, c_len), 1)
    causal = col <= row
    tril = causal.astype(BF16)

    @pl.when(pl.program_id(2) == 0)
    def _():
        st_ref[...] = jnp.zeros_like(st_ref)

    lbl = lbl_ref[...]
    e = jnp.exp(lbl - jnp.max(lbl, axis=0, keepdims=True))
    lb = e[0:1, :] / jnp.sum(e, axis=0, keepdims=True)

    q = _silu(q_ref[0].astype(F32))
    f = lb + (1.0 - lb) * jax.nn.sigmoid(f_ref[0].astype(F32))
    k = 1.0 - f
    log_f = jnp.log(f)
    chunks = [slice(c * c_len, (c + 1) * c_len) for c in range(ts // c_len)]
    hi = log_f.astype(BF16)
    lo = (log_f - hi.astype(F32)).astype(BF16)
    hilo = jnp.concatenate([hi, lo], axis=-1)
    cs = jnp.concatenate(
        [jnp.dot(tril, hilo[sl], preferred_element_type=F32) for sl in chunks], axis=0)
    b_cum = cs[:, :LANES] + cs[:, LANES:]
    log_decay = [b_cum[sl.stop - 1:sl.stop, :] for sl in chunks]
    decay = [jnp.exp(ld) for ld in log_decay]
    q_dec = (q * jnp.exp(b_cum)).astype(BF16)
    k_dec_f = k * jnp.exp(-b_cum)
    k_dec = k_dec_f.astype(BF16)
    k_end = jnp.concatenate([k_dec_f[sl] * decay[c] for c, sl in enumerate(chunks)], axis=0).astype(BF16)
    v = i_ref[0]
    gate = _silu(g_ref[0].astype(F32))
    gn = gn_ref[...]

    kv_t = [lax.dot_general(v[sl], k_end[sl], TN_DIMS, preferred_element_type=F32) for sl in chunks]
    a = [lax.dot_general(q_dec[sl], k_dec[sl], NT_DIMS, preferred_element_type=F32) for sl in chunks]
    a = [jnp.where(causal, a_c, 0.0).astype(BF16) for a_c in a]
    st = st_ref[...]
    entering = []
    for c in range(len(chunks)):
        entering.append(st.astype(BF16))
        st = st * decay[c] + kv_t[c]
    st_ref[...] = st
    o = jnp.concatenate(
        [jnp.dot(a[c], v[sl], preferred_element_type=F32)
         + lax.dot_general(q_dec[sl], entering[c], NT_DIMS, preferred_element_type=F32)
         for c, sl in enumerate(chunks)], axis=0)
    o = o * lax.rsqrt(jnp.mean(o * o, axis=-1, keepdims=True) + NORM_EPS) * gn
    o_ref[0] = (o * gate).astype(BF16)


def _hgrn(zh, lb_logits, gnorm, *, batch, seq, ts):
    nst = seq // ts

    def zspec(cb):
        return pl.BlockSpec((1, ts, LANES), lambda b, h, s: (cb + h, b * nst + s, 0))

    return pl.pallas_call(
        _hgrn_kernel,
        grid=(batch, HGRN_HEADS, nst),
        in_specs=[
            zspec(CB_HQ), zspec(CB_HF), zspec(CB_HI), zspec(CB_HG),
            pl.BlockSpec((lb_logits.shape[0], LANES), lambda b, h, s: (0, h)),
            pl.BlockSpec((1, LANES), lambda b, h, s: (0, 0)),
        ],
        out_specs=pl.BlockSpec((1, ts, LANES), lambda b, h, s: (h, b * nst + s, 0)),
        out_shape=jax.ShapeDtypeStruct((HGRN_HEADS, batch * seq, LANES), BF16),
        scratch_shapes=[pltpu.VMEM((HGRN_HEAD, HGRN_HEAD), F32)],
        compiler_params=pltpu.CompilerParams(
            dimension_semantics=("parallel", "parallel", "arbitrary"), vmem_limit_bytes=VMEM_LIMIT),
        name="hgrn2",
    )(zh, zh, zh, zh, lb_logits, gnorm)


def _unit_positions(p):
    rho = lax.broadcasted_iota(jnp.int32, (BAND, 2 * BAND), 0)
    kap = lax.broadcasted_iota(jnp.int32, (BAND, 2 * BAND), 1)
    if p == 2:
        return BAND + rho, kap
    if p == 1:
        q_pos = BAND + 4 * (rho & 31) + (rho >> 5)
        k_pos = (kap >> 7) * BAND + 4 * (kap & 31) + ((kap >> 5) & 3)
        return q_pos, k_pos
    q_pos = BAND + RES * (rho & 7) + (rho >> 3)
    k_pos = RES * (kap & 15) + (kap >> 4)
    return q_pos, k_pos


def _scores(q2, kb):
    return lax.dot_general(q2, kb, NT_DIMS, preferred_element_type=F32)


def _probs(s, bias0, bias1, head0):
    s0 = s[:BAND] + bias0
    s1 = s[BAND:] + bias1
    m0 = jnp.max(s0, axis=-1, keepdims=True)
    m1 = jnp.max(s1, axis=-1, keepdims=True)
    pc = jnp.concatenate([jnp.exp(s0 - m0), jnp.exp(s1 - m1)], axis=1).astype(BF16)
    return pc, jnp.where(head0, m0, m1)


def _attn_kernel(slopes_ref, q_ref, k_ref, v_ref, kp_ref, vp_ref, g_ref, o_ref,
                 bias_ref, qm_ref, vm_ref, qf_ref, kf_ref, vf_ref, acc_ref, mx_ref, den_ref, nat_ref):
    hp = pl.program_id(1)
    tile = pl.program_id(2)
    first_tile = jnp.where(tile == 0, 1, 0)
    lane = lax.broadcasted_iota(jnp.int32, (1, LANES), 1)
    head0 = lane < ATTN_HEAD
    prev_rows = SUBLANES

    @pl.when(tile == 0)
    def _():
        for p, (_, dil) in enumerate(DILATED_PATTERNS):
            q_pos, k_pos = _unit_positions(p)
            delta = q_pos - k_pos
            in_band = (delta >= 0) & (delta <= BAND)
            not_prev = k_pos >= BAND
            delta_f = delta.astype(F32)
            for hh in range(2):
                slope = slopes_ref[2 * hp + hh]
                bias = -(slope * float(dil)) * delta_f
                bias_ref[p, hh, 0] = jnp.where(in_band, bias, MASK_VALUE)
                bias_ref[p, hh, 1] = jnp.where(in_band & not_prev, bias, MASK_VALUE)

    qf = q_ref[0, 0].astype(F32) * (ATTN_HEAD ** -0.5)
    for hh in range(2):
        own = head0 if hh == 0 else jnp.logical_not(head0)
        qh = jnp.where(own, qf, 0.0)
        qf_ref[hh] = qh
        qm_ref[hh] = qh.astype(BF16)
        vcur = jnp.where(own, v_ref[0, 0].astype(F32), 0.0)
        vprev = jnp.where(own, vp_ref[0, 0].astype(F32), 0.0)
        vm_ref[hh, 1] = vcur.astype(BF16)
        vm_ref[hh, 0] = vprev.astype(BF16)
        vf_ref[hh, :, prev_rows:, :] = vcur
        vf_ref[hh, :, :prev_rows, :] = vprev[:, BAND - prev_rows:, :]
    kf_ref[:, prev_rows:, :] = k_ref[0, 0].astype(F32)
    kf_ref[:, :prev_rows, :] = kp_ref[0, 0, :, BAND - prev_rows:, :].astype(F32)

    key_row = lax.broadcasted_iota(jnp.int32, (4 * BAND, LANES), 0)
    key_lane = lax.broadcasted_iota(jnp.int32, (4 * BAND, LANES), 1)
    den_cols = ((key_row < 2 * BAND) == (key_lane < ATTN_HEAD)).astype(BF16)

    def pv(pc, v2):
        return jnp.dot(pc, jnp.concatenate([v2, den_cols], axis=1), preferred_element_type=F32)

    def run_group(p, gathers, variants, stores):
        s = [_scores(q2(), kb()) for q2, kb, _ in gathers]
        pm = [_probs(s_u, bias_ref[p, 0, var], bias_ref[p, 1, var], head0) for s_u, var in zip(s, variants)]
        for (pc, m_full), (_, _, v2), store in zip(pm, gathers, stores):
            od = pv(pc, v2())
            store(od[:, :LANES], od[:, LANES:], m_full)

    def group_d16(g, carry):
        gathers, stores = [], []
        for e in range(UNIT_GROUP):
            r = g * UNIT_GROUP + e
            gathers.append((
                lambda r=r: jnp.concatenate([qm_ref[0, r], qm_ref[1, r]], axis=0),
                lambda r=r: jnp.concatenate([kp_ref[0, 0, r], k_ref[0, 0, r]], axis=0),
                lambda r=r: jnp.concatenate([vm_ref[0, 0, r], vm_ref[0, 1, r],
                                             vm_ref[1, 0, r], vm_ref[1, 1, r]], axis=0)))

            def store(acc, den, m_full, r=r):
                acc_ref[2, r] = acc
                den_ref[2, r] = den
                mx_ref[2, r] = m_full
            stores.append(store)
        run_group(2, gathers, [first_tile] * UNIT_GROUP, stores)
        return carry

    lax.fori_loop(0, RES // UNIT_GROUP, group_d16, 0)

    rows4 = BAND // 4

    classes_per_group = UNIT_GROUP // 4

    def group_d4(g, carry):
        gathers, stores, variants = [], [], []
        for e in range(UNIT_GROUP):
            c = g * classes_per_group + e // 4
            n = e % 4
            cur = slice(n * rows4, (n + 1) * rows4)
            prv = slice((n - 1) * rows4, n * rows4) if n else slice(BAND - rows4, BAND)

            def q2(c=c, cur=cur):
                return jnp.concatenate([qm_ref[hh, 4 * a + c, cur] for hh in range(2) for a in range(4)], axis=0)

            def kb(c=c, cur=cur, prv=prv, n=n):
                src = k_ref if n else kp_ref
                return jnp.concatenate([src[0, 0, 4 * a + c, prv] for a in range(4)]
                                       + [k_ref[0, 0, 4 * a + c, cur] for a in range(4)], axis=0)

            def v2(c=c, cur=cur, prv=prv, n=n):
                return jnp.concatenate(
                    [vm_ref[hh, 1 if n else 0, 4 * a + c, prv] if part == 0 else vm_ref[hh, 1, 4 * a + c, cur]
                     for hh in range(2) for part in range(2) for a in range(4)], axis=0)

            def store(acc, den, m_full, c=c, cur=cur):
                for a in range(4):
                    piece = slice(a * rows4, (a + 1) * rows4)
                    acc_ref[1, 4 * a + c, cur] = acc[piece]
                    den_ref[1, 4 * a + c, cur] = den[piece]
                    mx_ref[1, 4 * a + c, cur] = m_full[piece]
            gathers.append((q2, kb, v2))
            stores.append(store)
            variants.append(first_tile if n == 0 else 0)
        run_group(1, gathers, variants, stores)
        return carry

    lax.fori_loop(0, 4 // classes_per_group, group_d4, 0)

    def group_d1(g, carry):
        gathers, stores, variants = [], [], []
        for e in range(UNIT_GROUP):
            n = g * UNIT_GROUP + e
            q_rows = pl.ds(pl.multiple_of(n * SUBLANES, SUBLANES), SUBLANES)
            k_rows = pl.ds(pl.multiple_of(n * SUBLANES, SUBLANES), 2 * SUBLANES)

            def q2(q_rows=q_rows):
                return jnp.concatenate(
                    [qf_ref[hh, :, q_rows, :].reshape(BAND, LANES) for hh in range(2)], axis=0).astype(BF16)

            def kb(k_rows=k_rows):
                return kf_ref[:, k_rows, :].reshape(2 * BAND, LANES).astype(BF16)

            def v2(k_rows=k_rows):
                return jnp.concatenate(
                    [vf_ref[hh, :, k_rows, :].reshape(2 * BAND, LANES) for hh in range(2)], axis=0).astype(BF16)

            def store(acc, den, m_full, q_rows=q_rows):
                acc_ref[0, :, q_rows, :] = acc.reshape(RES, SUBLANES, LANES)
                den_ref[0, :, q_rows, :] = den.reshape(RES, SUBLANES, LANES)
                mx_ref[0, :, q_rows, :] = m_full.reshape(RES, SUBLANES, LANES)
            gathers.append((q2, kb, v2))
            stores.append(store)
            variants.append(jnp.where((tile == 0) & (n == 0), 1, 0))
        run_group(0, gathers, variants, stores)
        return carry

    lax.fori_loop(0, BAND // SUBLANES // UNIT_GROUP, group_d1, 0)

    def combine(r, carry):
        m_all = jnp.maximum(jnp.maximum(mx_ref[0, r], mx_ref[1, r]), mx_ref[2, r])
        num = jnp.zeros((BAND, LANES), F32)
        den = jnp.zeros((BAND, LANES), F32)
        for p in range(len(DILATED_PATTERNS)):
            w = jnp.exp(mx_ref[p, r] - m_all)
            num = num + w * acc_ref[p, r]
            den = den + w * den_ref[p, r]
        gate = _silu(g_ref[0, 0, r].astype(F32))
        nat_ref[pl.ds(r, BAND, stride=RES), :] = num / den * gate
        return carry

    lax.fori_loop(0, RES, combine, 0)
    o_ref[0] = nat_ref[...].astype(BF16)


def _attn(za, slopes, *, batch, seq):
    nt_ = seq // ATTN_TILE
    n_pat = len(DILATED_PATTERNS)
    blk = (1, 1, RES, BAND, LANES)

    def zspec(cb):
        return pl.BlockSpec(blk, lambda b, h, t, sl: (cb + h, b * nt_ + t, 0, 0, 0))

    def zspec_prev(cb):
        return pl.BlockSpec(blk, lambda b, h, t, sl: (cb + h, b * nt_ + jnp.maximum(t - 1, 0), 0, 0, 0))

    grid_spec = pltpu.PrefetchScalarGridSpec(
        num_scalar_prefetch=1,
        grid=(batch, HEAD_PAIRS, nt_),
        in_specs=[zspec(CB_AQ), zspec(CB_AK), zspec(CB_AV), zspec_prev(CB_AK), zspec_prev(CB_AV), zspec(CB_AG)],
        out_specs=pl.BlockSpec((1, ATTN_TILE, LANES), lambda b, h, t, sl: (h, b * nt_ + t, 0)),
        scratch_shapes=[
            pltpu.VMEM((n_pat, 2, 2, BAND, 2 * BAND), F32),
            pltpu.VMEM((2, RES, BAND, LANES), BF16),
            pltpu.VMEM((2, 2, RES, BAND, LANES), BF16),
            pltpu.VMEM((2, RES, BAND, LANES), F32),
            pltpu.VMEM((RES, SUBLANES + BAND, LANES), F32),
            pltpu.VMEM((2, RES, SUBLANES + BAND, LANES), F32),
            pltpu.VMEM((n_pat, RES, BAND, LANES), F32),
            pltpu.VMEM((n_pat, RES, BAND, LANES), F32),
            pltpu.VMEM((n_pat, RES, BAND, LANES), F32),
            pltpu.VMEM((ATTN_TILE, LANES), F32),
        ],
    )
    return pl.pallas_call(
        _attn_kernel,
        grid_spec=grid_spec,
        out_shape=jax.ShapeDtypeStruct((HEAD_PAIRS, batch * seq, LANES), BF16),
        compiler_params=pltpu.CompilerParams(
            dimension_semantics=("parallel", "parallel", "arbitrary"), vmem_limit_bytes=VMEM_LIMIT),
        name="dilated_attn",
    )(slopes, za, za, za, za, za, za)


def _outproj_kernel(yh_ref, ya_ref, x_ref, w_ref, g_ref, o_ref):
    y = jnp.concatenate([yh_ref[c] for c in range(yh_ref.shape[0])]
                        + [ya_ref[c] for c in range(ya_ref.shape[0])], axis=-1)
    r = x_ref[...] + jnp.dot(y, w_ref[...], preferred_element_type=F32)
    ms = jnp.mean(r * r, axis=-1, keepdims=True)
    o_ref[...] = r * lax.rsqrt(ms + NORM_EPS) * g_ref[...]


def _outproj(yh3, ya3, x2, w_bf16, gain, *, tm):
    m, d = x2.shape
    return pl.pallas_call(
        _outproj_kernel,
        grid=(m // tm,),
        in_specs=[
            pl.BlockSpec((yh3.shape[0], tm, LANES), lambda i: (0, i, 0)),
            pl.BlockSpec((ya3.shape[0], tm, LANES), lambda i: (0, i, 0)),
            pl.BlockSpec((tm, d), lambda i: (i, 0)),
            pl.BlockSpec(w_bf16.shape, lambda i: (0, 0)),
            pl.BlockSpec((1, d), lambda i: (0, 0)),
        ],
        out_specs=pl.BlockSpec((tm, d), lambda i: (i, 0)),
        out_shape=jax.ShapeDtypeStruct((m, d), F32),
        compiler_params=pltpu.CompilerParams(
            dimension_semantics=("parallel",), vmem_limit_bytes=VMEM_LIMIT),
        name="outproj",
    )(yh3, ya3, x2, w_bf16, gain)


def kernel(x, norm_gain, w_in, lb_logits, hgrn_gnorm, w_out, final_gain):
    batch, seq, d = x.shape
    assert d == D_MODEL and w_in.shape == (1, D_MODEL, 4 * D_HGRN + 4 * D_ATTN) and lb_logits.shape[0] == 2
    m = batch * seq
    assert seq % ATTN_TILE == 0

    x2 = x.reshape(m, d)
    w_in_bf16 = w_in[0].astype(BF16)
    gain = norm_gain[0:1]
    zh = _inproj(x2, gain, w_in_bf16[:, :4 * D_HGRN], tm=1024, tn=1024, by_residue=False)
    za = _inproj(x2, gain, w_in_bf16[:, 4 * D_HGRN:], tm=1024, tn=1024, by_residue=True)
    yh3 = _hgrn(zh, lb_logits, hgrn_gnorm[0:1], batch=batch, seq=seq, ts=1024)
    slopes = jnp.exp2(-8.0 * jnp.arange(1, ATTN_HEADS + 1, dtype=F32) / ATTN_HEADS)
    ya3 = _attn(za, slopes, batch=batch, seq=seq)
    out = _outproj(yh3, ya3, x2, w_out[0].astype(BF16), final_gain.reshape(1, d), tm=512)
    return out.reshape(batch, seq, d)
```

```python
import functools

import jax
import jax.numpy as jnp
from jax import lax
from jax.experimental import pallas as pl
from jax.experimental.pallas import tpu as pltpu

D_MODEL = 2048
D_HGRN = 1024
HGRN_HEAD = 128
HGRN_HEADS = D_HGRN // HGRN_HEAD
HGRN_CHUNK = 64
D_ATTN = 1024
ATTN_HEAD = 64
ATTN_HEADS = D_ATTN // ATTN_HEAD
DILATED_PATTERNS = ((128, 1), (512, 4), (2048, 16))
NORM_EPS = 1e-6

LANES = 128
SUBLANES = 8
BAND = 128
HEAD_PAIRS = ATTN_HEADS // 2
MASK_VALUE = -1e30
RES = 16
ATTN_TILE = BAND * RES
UNIT_GROUP = 8

CB_HQ, CB_HF, CB_HI, CB_HG, CB_AG = 0, 8, 16, 24, 32
CB_AQ, CB_AK, CB_AV = 0, 8, 16

VMEM_LIMIT = 56 * 1024 * 1024

F32 = jnp.float32
BF16 = jnp.bfloat16
NT_DIMS = (((1,), (1,)), ((), ()))
TN_DIMS = (((0,), (0,)), ((), ()))


def _silu(v):
    return v * jax.nn.sigmoid(v)


def _inproj_kernel(*refs, row_chunk, by_residue, first_tile_scale):
    n_x = len(refs) - (5 if by_residue else 4)
    x_refs, (g_ref, w_ref, o_ref, h_ref) = refs[:n_x], refs[n_x:n_x + 4]
    tm, d = h_ref.shape
    per_res = tm // RES
    quarter = tm // 4

    @pl.when(pl.program_id(1) == 0)
    def _():
        if by_residue:
            y_ref = refs[-1]
            for c, x_ref in enumerate(x_refs):
                for c4 in range(4):
                    y_ref[c, c4 * quarter:(c4 + 1) * quarter, :] = x_ref[pl.ds(c4, quarter, stride=4), :]

            def residue(r, carry):
                src = pl.ds(lax.bitwise_and(r, 3) * quarter + lax.shift_right_logical(r, 2), per_res, stride=4)
                dst = pl.ds(pl.multiple_of(r * per_res, per_res), per_res)
                cols = [y_ref[c, src, :] for c in range(n_x)]
                ssq = cols[0] * cols[0]
                for xc in cols[1:]:
                    ssq = ssq + xc * xc
                scale = lax.rsqrt(jnp.sum(ssq, axis=-1, keepdims=True) * (1.0 / d) + NORM_EPS)
                for c, xc in enumerate(cols):
                    lanes = slice(c * LANES, (c + 1) * LANES)
                    h_ref[dst, lanes] = (xc * scale * g_ref[:, lanes]).astype(BF16)
                return carry

            lax.fori_loop(0, RES, residue, 0, unroll=4)
        else:
            x_ref, = x_refs
            for c in range(tm // row_chunk):
                rows = pl.ds(c * row_chunk, row_chunk)
                xv = x_ref[rows, :]
                ms = jnp.mean(xv * xv, axis=-1, keepdims=True)
                h_ref[rows, :] = (xv * lax.rsqrt(ms + NORM_EPS) * g_ref[...]).astype(BF16)

    col_scale = jnp.where(pl.program_id(1) == 0, first_tile_scale, 1.0) if by_residue else None
    for c in range(o_ref.shape[0] // 2):
        acc = jnp.dot(h_ref[...], w_ref[:, c * 2 * LANES:(c + 1) * 2 * LANES],
                      preferred_element_type=F32)
        if by_residue:
            acc = acc * col_scale
        for half in range(2):
            part = acc[:, half * LANES:(half + 1) * LANES].astype(BF16)
            if by_residue:
                for r in range(RES):
                    o_ref[2 * c + half, 0, r] = part[r * per_res:(r + 1) * per_res]
            else:
                o_ref[2 * c + half] = part


def _inproj(x2, gain, w_bf16, *, tm, tn, by_residue):
    m, d = x2.shape
    n = w_bf16.shape[1]
    if by_residue:
        halves = ATTN_TILE // tm
        out_shape = jax.ShapeDtypeStruct((n // LANES, m // ATTN_TILE, RES, BAND, LANES), BF16)
        out_spec = pl.BlockSpec((tn // LANES, 1, RES, tm // RES, LANES),
                                lambda i, j: (j, i // halves, 0, i % halves, 0))
        x_specs = [pl.BlockSpec((tm, LANES), lambda i, j, c=c: (i, c)) for c in range(d // LANES)]
    else:
        out_shape = jax.ShapeDtypeStruct((n // LANES, m, LANES), BF16)
        out_spec = pl.BlockSpec((tn // LANES, tm, LANES), lambda i, j: (j, i, 0))
        x_specs = [pl.BlockSpec((tm, d), lambda i, j: (i, 0))]
    scratch = [pltpu.VMEM((tm, d), BF16)]
    if by_residue:
        scratch.append(pltpu.VMEM((d // LANES, tm, LANES), F32))
    return pl.pallas_call(
        functools.partial(_inproj_kernel, row_chunk=256, by_residue=by_residue,
                          first_tile_scale=ATTN_HEAD ** -0.5),
        grid=(m // tm, n // tn),
        in_specs=x_specs + [
            pl.BlockSpec((1, d), lambda i, j: (0, 0)),
            pl.BlockSpec((d, tn), lambda i, j: (0, j)),
        ],
        out_specs=out_spec,
        out_shape=out_shape,
        scratch_shapes=scratch,
        compiler_params=pltpu.CompilerParams(
            dimension_semantics=("parallel", "arbitrary"), vmem_limit_bytes=VMEM_LIMIT),
        name="inproj_attn" if by_residue else "inproj_hgrn",
    )(*([x2] * len(x_specs)), gain, w_bf16)


def _hgrn_kernel(q_ref, f_ref, i_ref, g_ref, lbl_ref, gn_ref, o_ref, st_ref):
    ts = q_ref.shape[1]
    c_len = HGRN_CHUNK
    row = lax.broadcasted_iota(jnp.int32, (c_len, c_len), 0)
    col = lax.broadcasted_iota(jnp.int32, (c_len, c_len), 1)
    causal = col <= row
    tril = causal.astype(BF16)

    @pl.when(pl.program_id(2) == 0)
    def _():
        st_ref[...] = jnp.zeros_like(st_ref)

    lbl = lbl_ref[...]
    e = jnp.exp(lbl - jnp.max(lbl, axis=0, keepdims=True))
    lb = e[0:1, :] / jnp.sum(e, axis=0, keepdims=True)

    q = _silu(q_ref[0].astype(F32))
    f = lb + (1.0 - lb) * jax.nn.sigmoid(f_ref[0].astype(F32))
    k = 1.0 - f
    log_f = jnp.log(f)
    chunks = [slice(c * c_len, (c + 1) * c_len) for c in range(ts // c_len)]
    hi = log_f.astype(BF16)
    lo = (log_f - hi.astype(F32)).astype(BF16)
    hilo = jnp.concatenate([hi, lo], axis=-1)
    cs = jnp.concatenate(
        [jnp.dot(tril, hilo[sl], preferred_element_type=F32) for sl in chunks], axis=0)
    b_cum = cs[:, :LANES] + cs[:, LANES:]
    log_decay = [b_cum[sl.stop - 1:sl.stop, :] for sl in chunks]
    decay = [jnp.exp(ld) for ld in log_decay]
    q_dec = (q * jnp.exp(b_cum)).astype(BF16)
    k_dec_f = k * jnp.exp(-b_cum)
    k_dec = k_dec_f.astype(BF16)
    k_end = jnp.concatenate([k_dec_f[sl] * decay[c] for c, sl in enumerate(chunks)], axis=0).astype(BF16)
    v = i_ref[0]
    gate = _silu(g_ref[0].astype(F32))
    gn = gn_ref[...]

    kv_t = [lax.dot_general(v[sl], k_end[sl], TN_DIMS, preferred_element_type=F32) for sl in chunks]
    a = [lax.dot_general(q_dec[sl], k_dec[sl], NT_DIMS, preferred_element_type=F32) for sl in chunks]
    a = [jnp.where(causal, a_c, 0.0).astype(BF16) for a_c in a]
    st = st_ref[...]
    entering = []
    for c in range(len(chunks)):
        entering.append(st.astype(BF16))
        st = st * decay[c] + kv_t[c]
    st_ref[...] = st
    o = jnp.concatenate(
        [jnp.dot(a[c], v[sl], preferred_element_type=F32)
         + lax.dot_general(q_dec[sl], entering[c], NT_DIMS, preferred_element_type=F32)
         for c, sl in enumerate(chunks)], axis=0)
    o = o * lax.rsqrt(jnp.mean(o * o, axis=-1, keepdims=True) + NORM_EPS) * gn
    o_ref[0] = (o * gate).astype(BF16)


def _hgrn(zh, lb_logits, gnorm, *, batch, seq, ts):
    nst = seq // ts

    def zspec(cb):
        return pl.BlockSpec((1, ts, LANES), lambda b, h, s: (cb + h, b * nst + s, 0))

    return pl.pallas_call(
        _hgrn_kernel,
        grid=(batch, HGRN_HEADS, nst),
        in_specs=[
            zspec(CB_HQ), zspec(CB_HF), zspec(CB_HI), zspec(CB_HG),
            pl.BlockSpec((lb_logits.shape[0], LANES), lambda b, h, s: (0, h)),
            pl.BlockSpec((1, LANES), lambda b, h, s: (0, 0)),
        ],
        out_specs=pl.BlockSpec((1, ts, LANES), lambda b, h, s: (h, b * nst + s, 0)),
        out_shape=jax.ShapeDtypeStruct((HGRN_HEADS, batch * seq, LANES), BF16),
        scratch_shapes=[pltpu.VMEM((HGRN_HEAD, HGRN_HEAD), F32)],
        compiler_params=pltpu.CompilerParams(
            dimension_semantics=("parallel", "parallel", "arbitrary"), vmem_limit_bytes=VMEM_LIMIT),
        name="hgrn2",
    )(zh, zh, zh, zh, lb_logits, gnorm)


def _unit_positions(p):
    rho = lax.broadcasted_iota(jnp.int32, (BAND, 2 * BAND), 0)
    kap = lax.broadcasted_iota(jnp.int32, (BAND, 2 * BAND), 1)
    if p == 2:
        return BAND + rho, kap
    if p == 1:
        q_pos = BAND + 4 * (rho & 31) + (rho >> 5)
        k_pos = (kap >> 7) * BAND + 4 * (kap & 31) + ((kap >> 5) & 3)
        return q_pos, k_pos
    q_pos = BAND + RES * (rho & 7) + (rho >> 3)
    k_pos = RES * (kap & 15) + (kap >> 4)
    return q_pos, k_pos


def _scores(q2, kb):
    return lax.dot_general(q2, kb, NT_DIMS, preferred_element_type=F32)


def _probs(s, bias0, bias1, head0):
    s0 = s[:BAND] + bias0
    s1 = s[BAND:] + bias1
    m0 = jnp.max(s0, axis=-1, keepdims=True)
    m1 = jnp.max(s1, axis=-1, keepdims=True)
    pc = jnp.concatenate([jnp.exp(s0 - m0), jnp.exp(s1 - m1)], axis=1).astype(BF16)
    return pc, jnp.where(head0, m0, m1)


def _attn_kernel(slopes_ref, q_ref, k_ref, v_ref, kp_ref, g_ref, o_ref,
                 bias_ref, qm_ref, vm_ref, qf_ref, kf_ref, vf_ref, acc_ref, mx_ref, den_ref, nat_ref):
    hp = pl.program_id(1)
    tile = pl.program_id(2)
    first_tile = jnp.where(tile == 0, 1, 0)
    slot = lax.bitwise_and(tile, 1)
    lane = lax.broadcasted_iota(jnp.int32, (1, LANES), 1)
    head0 = lane < ATTN_HEAD
    prev_rows = SUBLANES

    @pl.when(tile == 0)
    def _():
        vm_ref[0, 1] = jnp.zeros((RES, BAND, LANES), BF16)
        vm_ref[1, 1] = jnp.zeros((RES, BAND, LANES), BF16)
        for p, (_, dil) in enumerate(DILATED_PATTERNS):
            q_pos, k_pos = _unit_positions(p)
            delta = q_pos - k_pos
            in_band = (delta >= 0) & (delta <= BAND)
            not_prev = k_pos >= BAND
            delta_f = delta.astype(F32)
            for hh in range(2):
                slope = slopes_ref[2 * hp + hh]
                bias = -(slope * float(dil)) * delta_f
                bias_ref[p, hh, 0] = jnp.where(in_band, bias, MASK_VALUE)
                bias_ref[p, hh, 1] = jnp.where(in_band & not_prev, bias, MASK_VALUE)

    def split_heads(x):
        zero = jnp.zeros_like(x)
        return jnp.where(head0, x, zero), jnp.where(head0, zero, x)

    q_bf = q_ref[0, 0]
    qf_ref[...] = q_bf.astype(F32)
    qm_ref[0], qm_ref[1] = split_heads(q_bf)
    packed_rows = 2 * SUBLANES
    v_tail = (vm_ref[0, 1 - slot, :, BAND - packed_rows:, :].astype(F32)
              + vm_ref[1, 1 - slot, :, BAND - packed_rows:, :].astype(F32))
    vf_ref[:, :prev_rows, :] = v_tail[:, packed_rows - prev_rows:, :]
    v_bf = v_ref[0, 0]
    vf_ref[:, prev_rows:, :] = v_bf.astype(F32)
    vm_ref[0, slot], vm_ref[1, slot] = split_heads(v_bf)
    k_tail = kp_ref[0, 0, :, BAND - packed_rows:, :].astype(F32)
    kf_ref[:, :prev_rows, :] = k_tail[:, packed_rows - prev_rows:, :]
    kf_ref[:, prev_rows:, :] = k_ref[0, 0].astype(F32)

    key_row = lax.broadcasted_iota(jnp.int32, (4 * BAND, LANES), 0)
    key_lane = lax.broadcasted_iota(jnp.int32, (4 * BAND, LANES), 1)
    den_cols = ((key_row < 2 * BAND) == (key_lane < ATTN_HEAD)).astype(BF16)

    def pv(pc, v2):
        return jnp.dot(pc, jnp.concatenate([v2, den_cols], axis=1), preferred_element_type=F32)

    def run_group(p, gathers, variants, stores):
        s = [_scores(q2(), kb()) for q2, kb, _ in gathers]
        pm = [_probs(s_u, bias_ref[p, 0, var], bias_ref[p, 1, var], head0) for s_u, var in zip(s, variants)]
        for (pc, m_full), (_, _, v2), store in zip(pm, gathers, stores):
            od = pv(pc, v2())
            store(od[:, :LANES], od[:, LANES:], m_full)

    def group_d16(g, carry):
        gathers, stores = [], []
        for e in range(UNIT_GROUP):
            r = g * UNIT_GROUP + e
            gathers.append((
                lambda r=r: jnp.concatenate([qm_ref[0, r], qm_ref[1, r]], axis=0),
                lambda r=r: jnp.concatenate([kp_ref[0, 0, r], k_ref[0, 0, r]], axis=0),
                lambda r=r: jnp.concatenate([vm_ref[0, 1 - slot, r], vm_ref[0, slot, r],
                                             vm_ref[1, 1 - slot, r], vm_ref[1, slot, r]], axis=0)))

            def store(acc, den, m_full, r=r):
                acc_ref[1, r] = acc
                den_ref[1, r] = den
                mx_ref[1, r] = m_full
            stores.append(store)
        run_group(2, gathers, [first_tile] * UNIT_GROUP, stores)
        return carry

    lax.fori_loop(0, RES // UNIT_GROUP, group_d16, 0)

    rows4 = BAND // 4

    classes_per_group = UNIT_GROUP // 4

    def group_d4(g, carry):
        gathers, stores, variants = [], [], []
        for e in range(UNIT_GROUP):
            c = g * classes_per_group + e // 4
            n = e % 4
            cur = slice(n * rows4, (n + 1) * rows4)
            prv = slice((n - 1) * rows4, n * rows4) if n else slice(BAND - rows4, BAND)

            def q2(c=c, cur=cur):
                return jnp.concatenate([qm_ref[hh, 4 * a + c, cur] for hh in range(2) for a in range(4)], axis=0)

            def kb(c=c, cur=cur, prv=prv, n=n):
                src = k_ref if n else kp_ref
                return jnp.concatenate([src[0, 0, 4 * a + c, prv] for a in range(4)]
                                       + [k_ref[0, 0, 4 * a + c, cur] for a in range(4)], axis=0)

            def v2(c=c, cur=cur, prv=prv, n=n):
                return jnp.concatenate(
                    [vm_ref[hh, slot if n else 1 - slot, 4 * a + c, prv] if part == 0
                     else vm_ref[hh, slot, 4 * a + c, cur]
                     for hh in range(2) for part in range(2) for a in range(4)], axis=0)

            def store(acc, den, m_full, c=c, cur=cur):
                for a in range(4):
                    piece = slice(a * rows4, (a + 1) * rows4)
                    acc_ref[0, 4 * a + c, cur] = acc[piece]
                    den_ref[0, 4 * a + c, cur] = den[piece]
                    mx_ref[0, 4 * a + c, cur] = m_full[piece]
            gathers.append((q2, kb, v2))
            stores.append(store)
            variants.append(first_tile if n == 0 else 0)
        run_group(1, gathers, variants, stores)
        return carry

    lax.fori_loop(0, 4 // classes_per_group, group_d4, 0)

    def group_d1(g, carry):
        gathers, stores, variants = [], [], []
        for e in range(UNIT_GROUP):
            n = g * UNIT_GROUP + e
            q_rows = pl.ds(pl.multiple_of(n * SUBLANES, SUBLANES), SUBLANES)
            k_rows = pl.ds(pl.multiple_of(n * SUBLANES, SUBLANES), 2 * SUBLANES)
            tok = pl.multiple_of(n * BAND, BAND)

            def q2(q_rows=q_rows):
                return jnp.concatenate(split_heads(qf_ref[:, q_rows, :].reshape(BAND, LANES)), axis=0).astype(BF16)

            def kb(k_rows=k_rows):
                return kf_ref[:, k_rows, :].reshape(2 * BAND, LANES).astype(BF16)

            def v2(k_rows=k_rows):
                return jnp.concatenate(split_heads(vf_ref[:, k_rows, :].reshape(2 * BAND, LANES)), axis=0).astype(BF16)

            def store(acc, den, m_full, q_rows=q_rows, tok=tok):
                others = [tuple(ref[p, :, q_rows, :].reshape(BAND, LANES) for ref in (acc_ref, den_ref, mx_ref))
                          for p in range(2)]
                m_all = jnp.maximum(jnp.maximum(m_full, others[0][2]), others[1][2])
                w = jnp.exp(m_full - m_all)
                num = w * acc
                total = w * den
                for acc_p, den_p, m_p in others:
                    w = jnp.exp(m_p - m_all)
                    num = num + w * acc_p
                    total = total + w * den_p
                y = (num / total).reshape(RES, SUBLANES, LANES)
                for r in range(RES):
                    nat_ref[pl.ds(tok + r, SUBLANES, stride=RES), :] = y[r]
                rows = pl.ds(tok, BAND)
                gate = _silu(g_ref[0, rows, :].astype(F32))
                o_ref[0, rows, :] = (nat_ref[rows, :] * gate).astype(BF16)
            gathers.append((q2, kb, v2))
            stores.append(store)
            variants.append(jnp.where((tile == 0) & (n == 0), 1, 0))
        run_group(0, gathers, variants, stores)
        return carry

    lax.fori_loop(0, BAND // SUBLANES // UNIT_GROUP, group_d1, 0)


def _attn(za, zh, slopes, *, batch, seq):
    nt_ = seq // ATTN_TILE
    n_pat = len(DILATED_PATTERNS)
    blk = (1, 1, RES, BAND, LANES)

    def zspec(cb):
        return pl.BlockSpec(blk, lambda b, h, t, sl: (cb + h, b * nt_ + t, 0, 0, 0))

    prev_keys = pl.BlockSpec(blk, lambda b, h, t, sl: (CB_AK + h, b * nt_ + jnp.maximum(t - 1, 0), 0, 0, 0))
    gate = pl.BlockSpec((1, ATTN_TILE, LANES), lambda b, h, t, sl: (CB_AG + h, b * nt_ + t, 0))
    grid_spec = pltpu.PrefetchScalarGridSpec(
        num_scalar_prefetch=1,
        grid=(batch, HEAD_PAIRS, nt_),
        in_specs=[zspec(CB_AQ), zspec(CB_AK), zspec(CB_AV), prev_keys, gate],
        out_specs=pl.BlockSpec((1, ATTN_TILE, LANES), lambda b, h, t, sl: (h, b * nt_ + t, 0)),
        scratch_shapes=[
            pltpu.VMEM((n_pat, 2, 2, BAND, 2 * BAND), F32),
            pltpu.VMEM((2, RES, BAND, LANES), BF16),
            pltpu.VMEM((2, 2, RES, BAND, LANES), BF16),
            pltpu.VMEM((RES, BAND, LANES), F32),
            pltpu.VMEM((RES, SUBLANES + BAND, LANES), F32),
            pltpu.VMEM((RES, SUBLANES + BAND, LANES), F32),
            pltpu.VMEM((n_pat - 1, RES, BAND, LANES), F32),
            pltpu.VMEM((n_pat - 1, RES, BAND, LANES), F32),
            pltpu.VMEM((n_pat - 1, RES, BAND, LANES), F32),
            pltpu.VMEM((ATTN_TILE, LANES), F32),
        ],
    )
    return pl.pallas_call(
        _attn_kernel,
        grid_spec=grid_spec,
        out_shape=jax.ShapeDtypeStruct((HEAD_PAIRS, batch * seq, LANES), BF16),
        compiler_params=pltpu.CompilerParams(
            dimension_semantics=("parallel", "parallel", "arbitrary"), vmem_limit_bytes=VMEM_LIMIT),
        name="dilated_attn",
    )(slopes, za, za, za, za, zh)


def _outproj_kernel(yh_ref, ya_ref, x_ref, w_ref, g_ref, o_ref):
    y = jnp.concatenate([yh_ref[c] for c in range(yh_ref.shape[0])]
                        + [ya_ref[c] for c in range(ya_ref.shape[0])], axis=-1)
    r = x_ref[...] + jnp.dot(y, w_ref[...], preferred_element_type=F32)
    ms = jnp.mean(r * r, axis=-1, keepdims=True)
    o_ref[...] = r * lax.rsqrt(ms + NORM_EPS) * g_ref[...]


def _outproj(yh3, ya3, x2, w_bf16, gain, *, tm):
    m, d = x2.shape
    return pl.pallas_call(
        _outproj_kernel,
        grid=(m // tm,),
        in_specs=[
            pl.BlockSpec((yh3.shape[0], tm, LANES), lambda i: (0, i, 0)),
            pl.BlockSpec((ya3.shape[0], tm, LANES), lambda i: (0, i, 0)),
            pl.BlockSpec((tm, d), lambda i: (i, 0)),
            pl.BlockSpec(w_bf16.shape, lambda i: (0, 0)),
            pl.BlockSpec((1, d), lambda i: (0, 0)),
        ],
        out_specs=pl.BlockSpec((tm, d), lambda i: (i, 0)),
        out_shape=jax.ShapeDtypeStruct((m, d), F32),
        compiler_params=pltpu.CompilerParams(
            dimension_semantics=("parallel",), vmem_limit_bytes=VMEM_LIMIT),
        name="outproj",
    )(yh3, ya3, x2, w_bf16, gain)


def kernel(x, norm_gain, w_in, lb_logits, hgrn_gnorm, w_out, final_gain):
    batch, seq, d = x.shape
    assert d == D_MODEL and w_in.shape == (1, D_MODEL, 4 * D_HGRN + 4 * D_ATTN) and lb_logits.shape[0] == 2
    m = batch * seq
    assert seq % ATTN_TILE == 0

    x2 = x.reshape(m, d)
    w_in_bf16 = w_in[0].astype(BF16)
    gain = norm_gain[0:1]
    qkv_cols = slice(4 * D_HGRN, 4 * D_HGRN + 3 * D_ATTN)
    w_tok = jnp.concatenate([w_in_bf16[:, :qkv_cols.start], w_in_bf16[:, qkv_cols.stop:]], axis=1)
    zh = _inproj(x2, gain, w_tok, tm=1024, tn=1024, by_residue=False)
    za = _inproj(x2, gain, w_in_bf16[:, qkv_cols], tm=1024, tn=1024, by_residue=True)
    yh3 = _hgrn(zh, lb_logits, hgrn_gnorm[0:1], batch=batch, seq=seq, ts=1024)
    slopes = jnp.exp2(-8.0 * jnp.arange(1, ATTN_HEADS + 1, dtype=F32) / ATTN_HEADS)
    ya3 = _attn(za, zh, slopes, batch=batch, seq=seq)
    out = _outproj(yh3, ya3, x2, w_out[0].astype(BF16), final_gain.reshape(1, d), tm=512)
    return out.reshape(batch, seq, d)
```

```python
import functools

import jax
import jax.numpy as jnp
from jax import lax
from jax.experimental import pallas as pl
from jax.experimental.pallas import tpu as pltpu

D_MODEL = 2048
D_HGRN = 1024
HGRN_HEAD = 128
HGRN_HEADS = D_HGRN // HGRN_HEAD
HGRN_CHUNK = 64
D_ATTN = 1024
ATTN_HEAD = 64
ATTN_HEADS = D_ATTN // ATTN_HEAD
DILATED_PATTERNS = ((128, 1), (512, 4), (2048, 16))
NORM_EPS = 1e-6

LANES = 128
SUBLANES = 8
BAND = 128
HEAD_PAIRS = ATTN_HEADS // 2
MASK_VALUE = -1e30
LOG2_E = 1.4426950408889634
RES = 16
ATTN_TILE = BAND * RES

CB_HQ, CB_HF, CB_HI, CB_HG, CB_AG = 0, 8, 16, 24, 32
CB_AQ, CB_AK, CB_AV = 0, 8, 16

VMEM_LIMIT = 56 * 1024 * 1024

F32 = jnp.float32
BF16 = jnp.bfloat16
NT_DIMS = (((1,), (1,)), ((), ()))
TN_DIMS = (((0,), (0,)), ((), ()))


def _sigmoid(v):
    return 0.5 * jnp.tanh(0.5 * v) + 0.5


def _silu(v):
    return v * _sigmoid(v)


def _inproj_kernel(*refs, row_chunk, by_residue, first_tile_scale):
    n_x = len(refs) - (5 if by_residue else 4)
    x_refs, (g_ref, w_ref, o_ref, h_ref) = refs[:n_x], refs[n_x:n_x + 4]
    tm, d = h_ref.shape
    per_res = tm // RES
    quarter = tm // 4

    @pl.when(pl.program_id(1) == 0)
    def _():
        if by_residue:
            y_ref = refs[-1]
            for c, x_ref in enumerate(x_refs):
                for c4 in range(4):
                    y_ref[c, c4 * quarter:(c4 + 1) * quarter, :] = x_ref[pl.ds(c4, quarter, stride=4), :]

            def residue(r, carry):
                src = pl.ds(lax.bitwise_and(r, 3) * quarter + lax.shift_right_logical(r, 2), per_res, stride=4)
                dst = pl.ds(pl.multiple_of(r * per_res, per_res), per_res)
                cols = [y_ref[c, src, :] for c in range(n_x)]
                ssq = cols[0] * cols[0]
                for xc in cols[1:]:
                    ssq = ssq + xc * xc
                scale = lax.rsqrt(jnp.sum(ssq, axis=-1, keepdims=True) * (1.0 / d) + NORM_EPS)
                for c, xc in enumerate(cols):
                    lanes = slice(c * LANES, (c + 1) * LANES)
                    h_ref[dst, lanes] = (xc * scale * g_ref[:, lanes]).astype(BF16)
                return carry

            lax.fori_loop(0, RES, residue, 0, unroll=4)
        else:
            x_ref, = x_refs
            for c in range(tm // row_chunk):
                rows = pl.ds(c * row_chunk, row_chunk)
                xv = x_ref[rows, :]
                ms = jnp.mean(xv * xv, axis=-1, keepdims=True)
                h_ref[rows, :] = (xv * lax.rsqrt(ms + NORM_EPS) * g_ref[...]).astype(BF16)

    col_scale = jnp.where(pl.program_id(1) == 0, first_tile_scale, 1.0) if by_residue else None
    for c in range(o_ref.shape[0] // 2):
        acc = jnp.dot(h_ref[...], w_ref[:, c * 2 * LANES:(c + 1) * 2 * LANES],
                      preferred_element_type=F32)
        if by_residue:
            acc = acc * col_scale
        for half in range(2):
            part = acc[:, half * LANES:(half + 1) * LANES].astype(BF16)
            if by_residue:
                for r in range(RES):
                    o_ref[2 * c + half, 0, r] = part[r * per_res:(r + 1) * per_res]
            else:
                o_ref[2 * c + half] = part


def _inproj(x2, gain, w_bf16, *, tm, tn, n_col_tiles, col_tile, by_residue):
    m, d = x2.shape
    n = n_col_tiles * tn
    if by_residue:
        halves = ATTN_TILE // tm
        out_shape = jax.ShapeDtypeStruct((n // LANES, m // ATTN_TILE, RES, BAND, LANES), BF16)
        out_spec = pl.BlockSpec((tn // LANES, 1, RES, tm // RES, LANES),
                                lambda i, j: (j, i // halves, 0, i % halves, 0))
        x_specs = [pl.BlockSpec((tm, LANES), lambda i, j, c=c: (i, c)) for c in range(d // LANES)]
    else:
        out_shape = jax.ShapeDtypeStruct((n // LANES, m, LANES), BF16)
        out_spec = pl.BlockSpec((tn // LANES, tm, LANES), lambda i, j: (j, i, 0))
        x_specs = [pl.BlockSpec((tm, d), lambda i, j: (i, 0))]
    scratch = [pltpu.VMEM((tm, d), BF16)]
    if by_residue:
        scratch.append(pltpu.VMEM((d // LANES, tm, LANES), F32))
    return pl.pallas_call(
        functools.partial(_inproj_kernel, row_chunk=256, by_residue=by_residue,
                          first_tile_scale=ATTN_HEAD ** -0.5 * LOG2_E),
        grid=(m // tm, n // tn),
        in_specs=x_specs + [
            pl.BlockSpec((1, d), lambda i, j: (0, 0)),
            pl.BlockSpec((d, tn), lambda i, j: (0, col_tile(j))),
        ],
        out_specs=out_spec,
        out_shape=out_shape,
        scratch_shapes=scratch,
        compiler_params=pltpu.CompilerParams(
            dimension_semantics=("parallel", "arbitrary"), vmem_limit_bytes=VMEM_LIMIT),
        name="inproj_attn" if by_residue else "inproj_hgrn",
    )(*([x2] * len(x_specs)), gain, w_bf16)


def _hgrn_kernel(q_ref, f_ref, i_ref, g_ref, lbl_ref, gn_ref, o_ref, st_ref):
    ts = q_ref.shape[1]
    c_len = HGRN_CHUNK
    row = lax.broadcasted_iota(jnp.int32, (c_len, c_len), 0)
    col = lax.broadcasted_iota(jnp.int32, (c_len, c_len), 1)
    causal = col <= row
    tril = causal.astype(BF16)

    @pl.when(pl.program_id(2) == 0)
    def _():
        st_ref[...] = jnp.zeros_like(st_ref)

    lbl = lbl_ref[...]
    e = jnp.exp(lbl - jnp.max(lbl, axis=0, keepdims=True))
    lb = e[0:1, :] / jnp.sum(e, axis=0, keepdims=True)

    q = _silu(q_ref[0].astype(F32))
    f = lb + (1.0 - lb) * _sigmoid(f_ref[0].astype(F32))
    k = 1.0 - f
    log_f = jnp.log(f)
    chunks = [slice(c * c_len, (c + 1) * c_len) for c in range(ts // c_len)]
    hi = log_f.astype(BF16)
    lo = (log_f - hi.astype(F32)).astype(BF16)
    hilo = jnp.concatenate([hi, lo], axis=-1)
    cs = jnp.concatenate(
        [jnp.dot(tril, hilo[sl], preferred_element_type=F32) for sl in chunks], axis=0)
    b_cum = cs[:, :LANES] + cs[:, LANES:]
    log_decay = [b_cum[sl.stop - 1:sl.stop, :] for sl in chunks]
    decay = [jnp.exp(ld) for ld in log_decay]
    q_dec = (q * jnp.exp(b_cum)).astype(BF16)
    k_dec_f = k * jnp.exp(-b_cum)
    k_dec = k_dec_f.astype(BF16)
    k_end = jnp.concatenate([k_dec_f[sl] * decay[c] for c, sl in enumerate(chunks)], axis=0).astype(BF16)
    v = i_ref[0]
    gate = _silu(g_ref[0].astype(F32))
    gn = gn_ref[...]

    kv_t = [lax.dot_general(v[sl], k_end[sl], TN_DIMS, preferred_element_type=F32) for sl in chunks]
    a = [lax.dot_general(q_dec[sl], k_dec[sl], NT_DIMS, preferred_element_type=F32) for sl in chunks]
    a = [jnp.where(causal, a_c, 0.0).astype(BF16) for a_c in a]
    st = st_ref[...]
    entering = []
    for c in range(len(chunks)):
        entering.append(st.astype(BF16))
        st = st * decay[c] + kv_t[c]
    st_ref[...] = st
    o = jnp.concatenate(
        [jnp.dot(a[c], v[sl], preferred_element_type=F32)
         + lax.dot_general(q_dec[sl], entering[c], NT_DIMS, preferred_element_type=F32)
         for c, sl in enumerate(chunks)], axis=0)
    o = o * lax.rsqrt(jnp.mean(o * o, axis=-1, keepdims=True) + NORM_EPS) * gn
    o_ref[0] = (o * gate).astype(BF16)


def _hgrn(zh, lb_logits, gnorm, *, batch, seq, ts):
    nst = seq // ts

    def zspec(cb):
        return pl.BlockSpec((1, ts, LANES), lambda b, h, s: (cb + h, b * nst + s, 0))

    return pl.pallas_call(
        _hgrn_kernel,
        grid=(batch, HGRN_HEADS, nst),
        in_specs=[
            zspec(CB_HQ), zspec(CB_HF), zspec(CB_HI), zspec(CB_HG),
            pl.BlockSpec((lb_logits.shape[0], LANES), lambda b, h, s: (0, h)),
            pl.BlockSpec((1, LANES), lambda b, h, s: (0, 0)),
        ],
        out_specs=pl.BlockSpec((1, ts, LANES), lambda b, h, s: (h, b * nst + s, 0)),
        out_shape=jax.ShapeDtypeStruct((HGRN_HEADS, batch * seq, LANES), BF16),
        scratch_shapes=[pltpu.VMEM((HGRN_HEAD, HGRN_HEAD), F32)],
        compiler_params=pltpu.CompilerParams(
            dimension_semantics=("parallel", "parallel", "arbitrary"), vmem_limit_bytes=VMEM_LIMIT),
        name="hgrn2",
    )(zh, zh, zh, zh, lb_logits, gnorm)


def _unit_positions(p):
    rho = lax.broadcasted_iota(jnp.int32, (BAND, 2 * BAND), 0)
    kap = lax.broadcasted_iota(jnp.int32, (BAND, 2 * BAND), 1)
    if p == 2:
        return BAND + rho, kap
    if p == 1:
        q_pos = BAND + 4 * (rho & 31) + (rho >> 5)
        k_pos = (kap >> 7) * BAND + 4 * (kap & 31) + ((kap >> 5) & 3)
        return q_pos, k_pos
    q_pos = BAND + RES * (rho & 7) + (rho >> 3)
    k_pos = RES * (kap & 15) + (kap >> 4)
    return q_pos, k_pos


def _scores(q2, kb):
    return lax.dot_general(q2, kb, NT_DIMS, preferred_element_type=F32)


def _probs(s, bias0, bias1, head0):
    s0 = s[:BAND] + bias0
    m0 = jnp.max(s0, axis=-1, keepdims=True)
    p0 = jnp.exp2(s0 - m0).astype(BF16)
    s1 = s[BAND:] + bias1
    m1 = jnp.max(s1, axis=-1, keepdims=True)
    p1 = jnp.exp2(s1 - m1).astype(BF16)
    return jnp.concatenate([p0, p1], axis=1), jnp.where(head0, m0, m1)


def _attn_kernel(slopes_ref, q_ref, k_ref, v_ref, kp_ref, g_ref, o_ref,
                 bias_ref, qm_ref, vm_ref, qf_ref, kf_ref, vf_ref, acc_ref, mx_ref, den_ref, nat_ref):
    hp = pl.program_id(1)
    tile = pl.program_id(2)
    first_tile = jnp.where(tile == 0, 1, 0)
    slot = lax.bitwise_and(tile, 1)
    lane = lax.broadcasted_iota(jnp.int32, (1, LANES), 1)
    head0 = lane < ATTN_HEAD
    prev_rows = SUBLANES

    @pl.when(tile == 0)
    def _():
        vm_ref[0, 1] = jnp.zeros((RES, BAND, LANES), BF16)
        vm_ref[1, 1] = jnp.zeros((RES, BAND, LANES), BF16)
        for p, (_, dil) in enumerate(DILATED_PATTERNS):
            q_pos, k_pos = _unit_positions(p)
            delta = q_pos - k_pos
            in_band = (delta >= 0) & (delta <= BAND)
            not_prev = k_pos >= BAND
            delta_f = delta.astype(F32)
            for hh in range(2):
                slope = slopes_ref[2 * hp + hh]
                bias = -(slope * (float(dil) * LOG2_E)) * delta_f
                bias_ref[p, hh, 0] = jnp.where(in_band, bias, MASK_VALUE)
                bias_ref[p, hh, 1] = jnp.where(in_band & not_prev, bias, MASK_VALUE)

    def split_heads(x):
        zero = jnp.zeros_like(x)
        return jnp.where(head0, x, zero), jnp.where(head0, zero, x)

    q_bf = q_ref[0, 0]
    qf_ref[...] = q_bf.astype(F32)
    qm_ref[0], qm_ref[1] = split_heads(q_bf)
    packed_rows = 2 * SUBLANES
    v_tail = (vm_ref[0, 1 - slot, :, BAND - packed_rows:, :].astype(F32)
              + vm_ref[1, 1 - slot, :, BAND - packed_rows:, :].astype(F32))
    vf_ref[:, :prev_rows, :] = v_tail[:, packed_rows - prev_rows:, :]
    v_bf = v_ref[0, 0]
    vf_ref[:, prev_rows:, :] = v_bf.astype(F32)
    vm_ref[0, slot], vm_ref[1, slot] = split_heads(v_bf)
    k_tail = kp_ref[0, 0, :, BAND - packed_rows:, :].astype(F32)
    kf_ref[:, :prev_rows, :] = k_tail[:, packed_rows - prev_rows:, :]
    kf_ref[:, prev_rows:, :] = k_ref[0, 0].astype(F32)

    key_row = lax.broadcasted_iota(jnp.int32, (4 * BAND, LANES), 0)
    key_lane = lax.broadcasted_iota(jnp.int32, (4 * BAND, LANES), 1)
    den_cols = ((key_row < 2 * BAND) == (key_lane < ATTN_HEAD)).astype(BF16)

    def pv(pc, v2):
        return jnp.dot(pc, jnp.concatenate([v2, den_cols], axis=1), preferred_element_type=F32)

    rows4 = BAND // 4

    def unit_d16(r):
        def q2():
            return jnp.concatenate([qm_ref[0, r], qm_ref[1, r]], axis=0)

        def kb():
            return jnp.concatenate([kp_ref[0, 0, r], k_ref[0, 0, r]], axis=0)

        def v2():
            return jnp.concatenate([vm_ref[0, 1 - slot, r], vm_ref[0, slot, r],
                                    vm_ref[1, 1 - slot, r], vm_ref[1, slot, r]], axis=0)

        def store(acc, den, m_full):
            acc_ref[1, r] = acc
            den_ref[1, r] = den
            mx_ref[1, r] = m_full
        return 2, first_tile, q2, kb, v2, store

    def unit_d4(c, n):
        cur = slice(n * rows4, (n + 1) * rows4)
        prv = slice((n - 1) * rows4, n * rows4) if n else slice(BAND - rows4, BAND)
        slabs = [4 * a + c for a in range(4)]

        def q2():
            return jnp.concatenate([qm_ref[hh, sb, cur] for hh in range(2) for sb in slabs], axis=0)

        def kb():
            src = k_ref if n else kp_ref
            return jnp.concatenate([src[0, 0, sb, prv] for sb in slabs] + [k_ref[0, 0, sb, cur] for sb in slabs], axis=0)

        def v2():
            prev_slot = slot if n else 1 - slot
            return jnp.concatenate(
                [vm_ref[hh, prev_slot, sb, prv] if part == 0 else vm_ref[hh, slot, sb, cur]
                 for hh in range(2) for part in range(2) for sb in slabs], axis=0)

        def store(acc, den, m_full):
            for a, sb in enumerate(slabs):
                piece = slice(a * rows4, (a + 1) * rows4)
                acc_ref[0, sb, cur] = acc[piece]
                den_ref[0, sb, cur] = den[piece]
                mx_ref[0, sb, cur] = m_full[piece]
        return 1, (first_tile if n == 0 else 0), q2, kb, v2, store

    def unit_d1(n):
        q_rows = slice(n * SUBLANES, (n + 1) * SUBLANES)
        k_rows = slice(n * SUBLANES, (n + 2) * SUBLANES)
        tok = n * BAND

        def q2():
            return jnp.concatenate(split_heads(qf_ref[:, q_rows, :].reshape(BAND, LANES)), axis=0).astype(BF16)

        def kb():
            return kf_ref[:, k_rows, :].reshape(2 * BAND, LANES).astype(BF16)

        def v2():
            return jnp.concatenate(split_heads(vf_ref[:, k_rows, :].reshape(2 * BAND, LANES)), axis=0).astype(BF16)

        def store(acc, den, m_full):
            others = [tuple(ref[p, :, q_rows, :].reshape(BAND, LANES) for ref in (acc_ref, den_ref, mx_ref))
                      for p in range(2)]
            m_all = jnp.maximum(jnp.maximum(m_full, others[0][2]), others[1][2])
            w = jnp.exp2(m_full - m_all)
            num = w * acc
            total = w * den
            for acc_p, den_p, m_p in others:
                w = jnp.exp2(m_p - m_all)
                num = num + w * acc_p
                total = total + w * den_p
            y = (num / total).reshape(RES, SUBLANES, LANES)
            for r in range(RES):
                nat_ref[pl.ds(tok + r, SUBLANES, stride=RES), :] = y[r]
            rows = slice(tok, tok + BAND)
            gate = _silu(g_ref[0, rows, :].astype(F32))
            o_ref[0, rows, :] = (nat_ref[rows, :] * gate).astype(BF16)
        return 0, (first_tile if n == 0 else 0), q2, kb, v2, store

    units = [unit_d16(r) for r in range(RES)]
    d1_ready = []
    for n4 in range(4):
        for c in range(4):
            units.append(unit_d4(c, n4))
            if d1_ready:
                units.append(unit_d1(d1_ready.pop(0)))
        d1_ready.extend(range(4 * n4, 4 * n4 + 4))
    units.extend(unit_d1(n) for n in d1_ready)

    s, pm = {}, {}
    for i in range(len(units) + 2):
        if i < len(units):
            s[i] = _scores(units[i][2](), units[i][3]())
        if 0 <= i - 1 < len(units):
            p, var = units[i - 1][:2]
            pm[i - 1] = _probs(s.pop(i - 1), bias_ref[p, 0, var], bias_ref[p, 1, var], head0)
        if 0 <= i - 2 < len(units):
            pc, m_full = pm.pop(i - 2)
            od = pv(pc, units[i - 2][4]())
            units[i - 2][5](od[:, :LANES], od[:, LANES:], m_full)


def _attn(za, zh, slopes, *, batch, seq):
    nt_ = seq // ATTN_TILE
    n_pat = len(DILATED_PATTERNS)
    blk = (1, 1, RES, BAND, LANES)

    def zspec(cb):
        return pl.BlockSpec(blk, lambda b, h, t, sl: (cb + h, b * nt_ + t, 0, 0, 0))

    prev_keys = pl.BlockSpec(blk, lambda b, h, t, sl: (CB_AK + h, b * nt_ + jnp.maximum(t - 1, 0), 0, 0, 0))
    gate = pl.BlockSpec((1, ATTN_TILE, LANES), lambda b, h, t, sl: (CB_AG + h, b * nt_ + t, 0))
    grid_spec = pltpu.PrefetchScalarGridSpec(
        num_scalar_prefetch=1,
        grid=(batch, HEAD_PAIRS, nt_),
        in_specs=[zspec(CB_AQ), zspec(CB_AK), zspec(CB_AV), prev_keys, gate],
        out_specs=pl.BlockSpec((1, ATTN_TILE, LANES), lambda b, h, t, sl: (h, b * nt_ + t, 0)),
        scratch_shapes=[
            pltpu.VMEM((n_pat, 2, 2, BAND, 2 * BAND), F32),
            pltpu.VMEM((2, RES, BAND, LANES), BF16),
            pltpu.VMEM((2, 2, RES, BAND, LANES), BF16),
            pltpu.VMEM((RES, BAND, LANES), F32),
            pltpu.VMEM((RES, SUBLANES + BAND, LANES), F32),
            pltpu.VMEM((RES, SUBLANES + BAND, LANES), F32),
            pltpu.VMEM((n_pat - 1, RES, BAND, LANES), F32),
            pltpu.VMEM((n_pat - 1, RES, BAND, LANES), F32),
            pltpu.VMEM((n_pat - 1, RES, BAND, LANES), F32),
            pltpu.VMEM((ATTN_TILE, LANES), F32),
        ],
    )
    return pl.pallas_call(
        _attn_kernel,
        grid_spec=grid_spec,
        out_shape=jax.ShapeDtypeStruct((HEAD_PAIRS, batch * seq, LANES), BF16),
        compiler_params=pltpu.CompilerParams(
            dimension_semantics=("parallel", "parallel", "arbitrary"), vmem_limit_bytes=VMEM_LIMIT),
        name="dilated_attn",
    )(slopes, za, za, za, za, zh)


def _outproj_kernel(yh_ref, ya_ref, x_ref, w_ref, g_ref, o_ref):
    y = jnp.concatenate([yh_ref[c] for c in range(yh_ref.shape[0])]
                        + [ya_ref[c] for c in range(ya_ref.shape[0])], axis=-1)
    r = x_ref[...] + jnp.dot(y, w_ref[...], preferred_element_type=F32)
    ms = jnp.mean(r * r, axis=-1, keepdims=True)
    o_ref[...] = r * lax.rsqrt(ms + NORM_EPS) * g_ref[...]


def _outproj(yh3, ya3, x2, w_bf16, gain, *, tm):
    m, d = x2.shape
    return pl.pallas_call(
        _outproj_kernel,
        grid=(m // tm,),
        in_specs=[
            pl.BlockSpec((yh3.shape[0], tm, LANES), lambda i: (0, i, 0)),
            pl.BlockSpec((ya3.shape[0], tm, LANES), lambda i: (0, i, 0)),
            pl.BlockSpec((tm, d), lambda i: (i, 0)),
            pl.BlockSpec(w_bf16.shape, lambda i: (0, 0)),
            pl.BlockSpec((1, d), lambda i: (0, 0)),
        ],
        out_specs=pl.BlockSpec((tm, d), lambda i: (i, 0)),
        out_shape=jax.ShapeDtypeStruct((m, d), F32),
        compiler_params=pltpu.CompilerParams(
            dimension_semantics=("parallel",), vmem_limit_bytes=VMEM_LIMIT),
        name="outproj",
    )(yh3, ya3, x2, w_bf16, gain)


def kernel(x, norm_gain, w_in, lb_logits, hgrn_gnorm, w_out, final_gain):
    batch, seq, d = x.shape
    assert d == D_MODEL and w_in.shape == (1, D_MODEL, 4 * D_HGRN + 4 * D_ATTN) and lb_logits.shape[0] == 2
    m = batch * seq
    assert seq % ATTN_TILE == 0

    x2 = x.reshape(m, d)
    w_in_bf16 = w_in[0].astype(BF16)
    gain = norm_gain[0:1]
    zh = _inproj(x2, gain, w_in_bf16, tm=1024, tn=D_HGRN, n_col_tiles=5,
                 col_tile=lambda j: jnp.where(j < 4, j, 7), by_residue=False)
    za = _inproj(x2, gain, w_in_bf16, tm=1024, tn=D_ATTN, n_col_tiles=3,
                 col_tile=lambda j: j + 4, by_residue=True)
    yh3 = _hgrn(zh, lb_logits, hgrn_gnorm[0:1], batch=batch, seq=seq, ts=1024)
    slopes = jnp.exp2(-8.0 * jnp.arange(1, ATTN_HEADS + 1, dtype=F32) / ATTN_HEADS)
    ya3 = _attn(za, zh, slopes, batch=batch, seq=seq)
    out = _outproj(yh3, ya3, x2, w_out[0].astype(BF16), final_gain.reshape(1, d), tm=512)
    return out.reshape(batch, seq, d)
```

```python
import functools

import jax
import jax.numpy as jnp
from jax import lax
from jax.experimental import pallas as pl
from jax.experimental.pallas import tpu as pltpu

D_MODEL = 2048
D_HGRN = 1024
HGRN_HEAD = 128
HGRN_HEADS = D_HGRN // HGRN_HEAD
HGRN_CHUNK = 64
D_ATTN = 1024
ATTN_HEAD = 64
ATTN_HEADS = D_ATTN // ATTN_HEAD
DILATED_PATTERNS = ((128, 1), (512, 4), (2048, 16))
NORM_EPS = 1e-6

LANES = 128
SUBLANES = 8
BAND = 128
HEAD_PAIRS = ATTN_HEADS // 2
MASK_VALUE = -1e30
LOG2_E = 1.4426950408889634
RES = 16
ATTN_TILE = BAND * RES

CB_HQ, CB_HF, CB_HI, CB_HG, CB_AG = 0, 8, 16, 24, 32
CB_AQ, CB_AK, CB_AV = 0, 8, 16

VMEM_LIMIT = 56 * 1024 * 1024

F32 = jnp.float32
BF16 = jnp.bfloat16
NT_DIMS = (((1,), (1,)), ((), ()))
TN_DIMS = (((0,), (0,)), ((), ()))


def _sigmoid(v):
    return 0.5 * jnp.tanh(0.5 * v) + 0.5


def _silu(v):
    return v * _sigmoid(v)


def _inproj_kernel(*refs, by_residue, first_tile_scale):
    n_x = len(refs) - (5 if by_residue else 4)
    x_refs, (g_ref, w_ref, o_ref, h_ref) = refs[:n_x], refs[n_x:n_x + 4]
    tm, d = h_ref.shape
    per_res = tm // RES
    chunk = tm // 4
    j = pl.program_id(1)

    def project(h_rows, row0, scale):
        n_rows = h_rows.shape[0]
        for c in range(o_ref.shape[0] // 2):
            acc = jnp.dot(h_rows, w_ref[:, c * 2 * LANES:(c + 1) * 2 * LANES], preferred_element_type=F32)
            if scale is not None:
                acc = acc * scale
            for half in range(2):
                part = acc[:, half * LANES:(half + 1) * LANES].astype(BF16)
                if by_residue:
                    for k in range(n_rows // per_res):
                        blk = row0 // per_res + k
                        o_ref[2 * c + half, 0, 4 * (blk % 4) + blk // 4] = part[k * per_res:(k + 1) * per_res]
                else:
                    o_ref[2 * c + half, row0:row0 + n_rows] = part

    @pl.when(j == 0)
    def _():
        for c4 in range(4):
            rows = slice(c4 * chunk, (c4 + 1) * chunk)
            if by_residue:
                y_ref = refs[-1]
                for c, x_ref in enumerate(x_refs):
                    y_ref[c, rows, :] = x_ref[pl.ds(c4, chunk, stride=4), :]
                for a in range(4):
                    src = pl.ds(c4 * chunk + a, per_res, stride=4)
                    dst = slice((4 * c4 + a) * per_res, (4 * c4 + a + 1) * per_res)
                    cols = [y_ref[c, src, :] for c in range(n_x)]
                    ssq = cols[0] * cols[0]
                    for xc in cols[1:]:
                        ssq = ssq + xc * xc
                    scale = lax.rsqrt(jnp.sum(ssq, axis=-1, keepdims=True) * (1.0 / d) + NORM_EPS)
                    for c, xc in enumerate(cols):
                        lanes = slice(c * LANES, (c + 1) * LANES)
                        h_ref[dst, lanes] = (xc * scale * g_ref[:, lanes]).astype(BF16)
            else:
                xv = x_refs[0][rows, :]
                ms = jnp.mean(xv * xv, axis=-1, keepdims=True)
                h_ref[rows, :] = (xv * lax.rsqrt(ms + NORM_EPS) * g_ref[...]).astype(BF16)
            project(h_ref[rows, :], c4 * chunk, first_tile_scale if by_residue else None)

    @pl.when(j > 0)
    def _():
        project(h_ref[...], 0, None)


def _inproj(x2, gain, w_bf16, *, tm, tn, n_col_tiles, col_tile, by_residue):
    m, d = x2.shape
    n = n_col_tiles * tn
    if by_residue:
        halves = ATTN_TILE // tm
        out_shape = jax.ShapeDtypeStruct((n // LANES, m // ATTN_TILE, RES, BAND, LANES), BF16)
        out_spec = pl.BlockSpec((tn // LANES, 1, RES, tm // RES, LANES),
                                lambda i, j: (j, i // halves, 0, i % halves, 0))
        x_specs = [pl.BlockSpec((tm, LANES), lambda i, j, c=c: (i, c)) for c in range(d // LANES)]
    else:
        out_shape = jax.ShapeDtypeStruct((n // LANES, m, LANES), BF16)
        out_spec = pl.BlockSpec((tn // LANES, tm, LANES), lambda i, j: (j, i, 0))
        x_specs = [pl.BlockSpec((tm, d), lambda i, j: (i, 0))]
    scratch = [pltpu.VMEM((tm, d), BF16)]
    if by_residue:
        scratch.append(pltpu.VMEM((d // LANES, tm, LANES), F32))
    return pl.pallas_call(
        functools.partial(_inproj_kernel, by_residue=by_residue,
                          first_tile_scale=ATTN_HEAD ** -0.5 * LOG2_E),
        grid=(m // tm, n // tn),
        in_specs=x_specs + [
            pl.BlockSpec((1, d), lambda i, j: (0, 0)),
            pl.BlockSpec((d, tn), lambda i, j: (0, col_tile(j))),
        ],
        out_specs=out_spec,
        out_shape=out_shape,
        scratch_shapes=scratch,
        compiler_params=pltpu.CompilerParams(
            dimension_semantics=("parallel", "arbitrary"), vmem_limit_bytes=VMEM_LIMIT),
        name="inproj_attn" if by_residue else "inproj_hgrn",
    )(*([x2] * len(x_specs)), gain, w_bf16)


def _hgrn_kernel(q_ref, f_ref, i_ref, g_ref, lbl_ref, gn_ref, o_ref, st_ref):
    nh, ts, _ = q_ref.shape
    heads = range(nh)
    c_len = HGRN_CHUNK
    chunks = [slice(c * c_len, (c + 1) * c_len) for c in range(ts // c_len)]
    row = lax.broadcasted_iota(jnp.int32, (c_len, c_len), 0)
    col = lax.broadcasted_iota(jnp.int32, (c_len, c_len), 1)
    causal = col <= row
    tril = causal.astype(BF16)

    @pl.when(pl.program_id(2) == 0)
    def _():
        st_ref[...] = jnp.zeros_like(st_ref)

    lbl = lbl_ref[...]
    e = jnp.exp(lbl - jnp.max(lbl, axis=0, keepdims=True))
    lb_all = e[0:1, :] / jnp.sum(e, axis=0, keepdims=True)
    lb = [lb_all[:, h * LANES:(h + 1) * LANES] for h in heads]

    q = [_silu(q_ref[h].astype(F32)) for h in heads]
    f = [lb[h] + (1.0 - lb[h]) * _sigmoid(f_ref[h].astype(F32)) for h in heads]
    k = [1.0 - f[h] for h in heads]
    log_f = [jnp.log(f[h]) for h in heads]
    hilo = []
    for h in heads:
        hi = log_f[h].astype(BF16)
        lo = (log_f[h] - hi.astype(F32)).astype(BF16)
        hilo.append(jnp.concatenate([hi, lo], axis=-1))
    cs = [jnp.concatenate([jnp.dot(tril, hilo[h][sl], preferred_element_type=F32) for sl in chunks], axis=0)
          for h in heads]
    b_cum = [cs[h][:, :LANES] + cs[h][:, LANES:] for h in heads]
    decay = [[jnp.exp(b_cum[h][sl.stop - 1:sl.stop, :]) for sl in chunks] for h in heads]
    q_dec = [(q[h] * jnp.exp(b_cum[h])).astype(BF16) for h in heads]
    k_dec_f = [k[h] * jnp.exp(-b_cum[h]) for h in heads]
    k_dec = [k_dec_f[h].astype(BF16) for h in heads]
    k_end = [jnp.concatenate([k_dec_f[h][sl] * decay[h][c] for c, sl in enumerate(chunks)], axis=0).astype(BF16)
             for h in heads]
    v = [i_ref[h] for h in heads]
    gn = gn_ref[...]

    kv_t = [[lax.dot_general(v[h][sl], k_end[h][sl], TN_DIMS, preferred_element_type=F32) for sl in chunks]
            for h in heads]
    a = [[lax.dot_general(q_dec[h][sl], k_dec[h][sl], NT_DIMS, preferred_element_type=F32) for sl in chunks]
         for h in heads]
    a = [[jnp.where(causal, a_c, 0.0).astype(BF16) for a_c in a[h]] for h in heads]
    entering = [[] for _ in heads]
    for h in heads:
        st = st_ref[h]
        for c in range(len(chunks)):
            entering[h].append(st.astype(BF16))
            st = st * decay[h][c] + kv_t[h][c]
        st_ref[h] = st
    for h in heads:
        o = jnp.concatenate(
            [jnp.dot(a[h][c], v[h][sl], preferred_element_type=F32)
             + lax.dot_general(q_dec[h][sl], entering[h][c], NT_DIMS, preferred_element_type=F32)
             for c, sl in enumerate(chunks)], axis=0)
        o = o * lax.rsqrt(jnp.mean(o * o, axis=-1, keepdims=True) + NORM_EPS) * gn
        o_ref[h] = (o * _silu(g_ref[h].astype(F32))).astype(BF16)


def _hgrn(zh, lb_logits, gnorm, *, batch, seq, ts, nh):
    nst = seq // ts

    def zspec(cb):
        return pl.BlockSpec((nh, ts, LANES), lambda b, h, s: (cb // nh + h, b * nst + s, 0))

    return pl.pallas_call(
        _hgrn_kernel,
        grid=(batch, HGRN_HEADS // nh, nst),
        in_specs=[
            zspec(CB_HQ), zspec(CB_HF), zspec(CB_HI), zspec(CB_HG),
            pl.BlockSpec((lb_logits.shape[0], nh * LANES), lambda b, h, s: (0, h)),
            pl.BlockSpec((1, LANES), lambda b, h, s: (0, 0)),
        ],
        out_specs=pl.BlockSpec((nh, ts, LANES), lambda b, h, s: (h, b * nst + s, 0)),
        out_shape=jax.ShapeDtypeStruct((HGRN_HEADS, batch * seq, LANES), BF16),
        scratch_shapes=[pltpu.VMEM((nh, HGRN_HEAD, HGRN_HEAD), F32)],
        compiler_params=pltpu.CompilerParams(
            dimension_semantics=("parallel", "parallel", "arbitrary"), vmem_limit_bytes=VMEM_LIMIT),
        name="hgrn2",
    )(zh, zh, zh, zh, lb_logits, gnorm)


def _unit_positions(p):
    rho = lax.broadcasted_iota(jnp.int32, (BAND, 2 * BAND), 0)
    kap = lax.broadcasted_iota(jnp.int32, (BAND, 2 * BAND), 1)
    if p == 2:
        return BAND + rho, kap
    if p == 1:
        q_pos = BAND + 4 * (rho & 31) + (rho >> 5)
        k_pos = (kap >> 7) * BAND + 4 * (kap & 31) + ((kap >> 5) & 3)
        return q_pos, k_pos
    q_pos = BAND + RES * (rho & 7) + (rho >> 3)
    k_pos = RES * (kap & 15) + (kap >> 4)
    return q_pos, k_pos


def _scores(q2, kb):
    return lax.dot_general(q2, kb, NT_DIMS, preferred_element_type=F32)


def _probs(s, bias0, bias1, head0):
    s0 = s[:BAND] + bias0
    m0 = jnp.max(s0, axis=-1, keepdims=True)
    p0 = jnp.exp2(s0 - m0).astype(BF16)
    s1 = s[BAND:] + bias1
    m1 = jnp.max(s1, axis=-1, keepdims=True)
    p1 = jnp.exp2(s1 - m1).astype(BF16)
    return jnp.concatenate([p0, p1], axis=1), jnp.where(head0, m0, m1)


def _attn_kernel(slopes_ref, q_ref, k_ref, v_ref, kp_ref, g_ref, o_ref,
                 bias_ref, qm_ref, vm_ref, qf_ref, kf_ref, vf_ref, acc_ref, mx_ref, den_ref, nat_ref):
    hp = pl.program_id(1)
    tile = pl.program_id(2)
    first_tile = jnp.where(tile == 0, 1, 0)
    slot = lax.bitwise_and(tile, 1)
    lane = lax.broadcasted_iota(jnp.int32, (1, LANES), 1)
    head0 = lane < ATTN_HEAD
    prev_rows = SUBLANES

    @pl.when(tile == 0)
    def _():
        vm_ref[0, 1] = jnp.zeros((RES, BAND, LANES), BF16)
        vm_ref[1, 1] = jnp.zeros((RES, BAND, LANES), BF16)
        for p, (_, dil) in enumerate(DILATED_PATTERNS):
            q_pos, k_pos = _unit_positions(p)
            delta = q_pos - k_pos
            in_band = (delta >= 0) & (delta <= BAND)
            not_prev = k_pos >= BAND
            delta_f = delta.astype(F32)
            for hh in range(2):
                slope = slopes_ref[2 * hp + hh]
                bias = -(slope * (float(dil) * LOG2_E)) * delta_f
                bias_ref[p, hh, 0] = jnp.where(in_band, bias, MASK_VALUE)
                bias_ref[p, hh, 1] = jnp.where(in_band & not_prev, bias, MASK_VALUE)

    def split_heads(x):
        zero = jnp.zeros_like(x)
        return jnp.where(head0, x, zero), jnp.where(head0, zero, x)

    q_bf = q_ref[0, 0]
    qf_ref[...] = q_bf.astype(F32)
    qm_ref[0], qm_ref[1] = split_heads(q_bf)
    packed_rows = 2 * SUBLANES
    v_tail = (vm_ref[0, 1 - slot, :, BAND - packed_rows:, :].astype(F32)
              + vm_ref[1, 1 - slot, :, BAND - packed_rows:, :].astype(F32))
    vf_ref[:, :prev_rows, :] = v_tail[:, packed_rows - prev_rows:, :]
    v_bf = v_ref[0, 0]
    vf_ref[:, prev_rows:, :] = v_bf.astype(F32)
    vm_ref[0, slot], vm_ref[1, slot] = split_heads(v_bf)
    k_tail = kp_ref[0, 0, :, BAND - packed_rows:, :].astype(F32)
    kf_ref[:, :prev_rows, :] = k_tail[:, packed_rows - prev_rows:, :]
    kf_ref[:, prev_rows:, :] = k_ref[0, 0].astype(F32)

    key_row = lax.broadcasted_iota(jnp.int32, (4 * BAND, LANES), 0)
    key_lane = lax.broadcasted_iota(jnp.int32, (4 * BAND, LANES), 1)
    den_cols = ((key_row < 2 * BAND) == (key_lane < ATTN_HEAD)).astype(BF16)

    def pv(pc, v2):
        return jnp.dot(pc, jnp.concatenate([v2, den_cols], axis=1), preferred_element_type=F32)

    rows4 = BAND // 4

    def unit_d16(r):
        def q2():
            return jnp.concatenate([qm_ref[0, r], qm_ref[1, r]], axis=0)

        def kb():
            return jnp.concatenate([kp_ref[0, 0, r], k_ref[0, 0, r]], axis=0)

        def v2():
            return jnp.concatenate([vm_ref[0, 1 - slot, r], vm_ref[0, slot, r],
                                    vm_ref[1, 1 - slot, r], vm_ref[1, slot, r]], axis=0)

        def store(acc, den, m_full):
            acc_ref[1, r] = acc
            den_ref[1, r] = den
            mx_ref[1, r] = m_full
        return 2, first_tile, q2, kb, v2, store

    def unit_d4(c, n):
        cur = slice(n * rows4, (n + 1) * rows4)
        prv = slice((n - 1) * rows4, n * rows4) if n else slice(BAND - rows4, BAND)
        slabs = [4 * a + c for a in range(4)]

        def q2():
            return jnp.concatenate([qm_ref[hh, sb, cur] for hh in range(2) for sb in slabs], axis=0)

        def kb():
            src = k_ref if n else kp_ref
            return jnp.concatenate([src[0, 0, sb, prv] for sb in slabs] + [k_ref[0, 0, sb, cur] for sb in slabs], axis=0)

        def v2():
            prev_slot = slot if n else 1 - slot
            return jnp.concatenate(
                [vm_ref[hh, prev_slot, sb, prv] if part == 0 else vm_ref[hh, slot, sb, cur]
                 for hh in range(2) for part in range(2) for sb in slabs], axis=0)

        def store(acc, den, m_full):
            for a, sb in enumerate(slabs):
                piece = slice(a * rows4, (a + 1) * rows4)
                acc_ref[0, sb, cur] = acc[piece]
                den_ref[0, sb, cur] = den[piece]
                mx_ref[0, sb, cur] = m_full[piece]
        return 1, (first_tile if n == 0 else 0), q2, kb, v2, store

    def unit_d1(n):
        q_rows = slice(n * SUBLANES, (n + 1) * SUBLANES)
        k_rows = slice(n * SUBLANES, (n + 2) * SUBLANES)
        tok = n * BAND

        def q2():
            return jnp.concatenate(split_heads(qf_ref[:, q_rows, :].reshape(BAND, LANES)), axis=0).astype(BF16)

        def kb():
            return kf_ref[:, k_rows, :].reshape(2 * BAND, LANES).astype(BF16)

        def v2():
            return jnp.concatenate(split_heads(vf_ref[:, k_rows, :].reshape(2 * BAND, LANES)), axis=0).astype(BF16)

        def store(acc, den, m_full):
            others = [tuple(ref[p, :, q_rows, :].reshape(BAND, LANES) for ref in (acc_ref, den_ref, mx_ref))
                      for p in range(2)]
            m_all = jnp.maximum(jnp.maximum(m_full, others[0][2]), others[1][2])
            w = jnp.exp2(m_full - m_all)
            num = w * acc
            total = w * den
            for acc_p, den_p, m_p in others:
                w = jnp.exp2(m_p - m_all)
                num = num + w * acc_p
                total = total + w * den_p
            y = (num / total).reshape(RES, SUBLANES, LANES)
            for r in range(RES):
                nat_ref[pl.ds(tok + r, SUBLANES, stride=RES), :] = y[r]
            rows = slice(tok, tok + BAND)
            gate = _silu(g_ref[0, rows, :].astype(F32))
            o_ref[0, rows, :] = (nat_ref[rows, :] * gate).astype(BF16)
        return 0, (first_tile if n == 0 else 0), q2, kb, v2, store

    units = [unit_d16(r) for r in range(RES)]
    d1_ready = []
    for n4 in range(4):
        for c in range(4):
            units.append(unit_d4(c, n4))
            if d1_ready:
                units.append(unit_d1(d1_ready.pop(0)))
        d1_ready.extend(range(4 * n4, 4 * n4 + 4))
    units.extend(unit_d1(n) for n in d1_ready)

    s, pm = {}, {}
    for i in range(len(units) + 2):
        if i < len(units):
            s[i] = _scores(units[i][2](), units[i][3]())
        if 0 <= i - 1 < len(units):
            p, var = units[i - 1][:2]
            pm[i - 1] = _probs(s.pop(i - 1), bias_ref[p, 0, var], bias_ref[p, 1, var], head0)
        if 0 <= i - 2 < len(units):
            pc, m_full = pm.pop(i - 2)
            od = pv(pc, units[i - 2][4]())
            units[i - 2][5](od[:, :LANES], od[:, LANES:], m_full)


def _attn(za, zh, slopes, *, batch, seq):
    nt_ = seq // ATTN_TILE
    n_pat = len(DILATED_PATTERNS)
    blk = (1, 1, RES, BAND, LANES)

    def zspec(cb):
        return pl.BlockSpec(blk, lambda b, h, t, sl: (cb + h, b * nt_ + t, 0, 0, 0))

    prev_keys = pl.BlockSpec(blk, lambda b, h, t, sl: (CB_AK + h, b * nt_ + jnp.maximum(t - 1, 0), 0, 0, 0))
    gate = pl.BlockSpec((1, ATTN_TILE, LANES), lambda b, h, t, sl: (CB_AG + h, b * nt_ + t, 0))
    grid_spec = pltpu.PrefetchScalarGridSpec(
        num_scalar_prefetch=1,
        grid=(batch, HEAD_PAIRS, nt_),
        in_specs=[zspec(CB_AQ), zspec(CB_AK), zspec(CB_AV), prev_keys, gate],
        out_specs=pl.BlockSpec((1, ATTN_TILE, LANES), lambda b, h, t, sl: (h, b * nt_ + t, 0)),
        scratch_shapes=[
            pltpu.VMEM((n_pat, 2, 2, BAND, 2 * BAND), F32),
            pltpu.VMEM((2, RES, BAND, LANES), BF16),
            pltpu.VMEM((2, 2, RES, BAND, LANES), BF16),
            pltpu.VMEM((RES, BAND, LANES), F32),
            pltpu.VMEM((RES, SUBLANES + BAND, LANES), F32),
            pltpu.VMEM((RES, SUBLANES + BAND, LANES), F32),
            pltpu.VMEM((n_pat - 1, RES, BAND, LANES), F32),
            pltpu.VMEM((n_pat - 1, RES, BAND, LANES), F32),
            pltpu.VMEM((n_pat - 1, RES, BAND, LANES), F32),
            pltpu.VMEM((ATTN_TILE, LANES), F32),
        ],
    )
    return pl.pallas_call(
        _attn_kernel,
        grid_spec=grid_spec,
        out_shape=jax.ShapeDtypeStruct((HEAD_PAIRS, batch * seq, LANES), BF16),
        compiler_params=pltpu.CompilerParams(
            dimension_semantics=("parallel", "parallel", "arbitrary"), vmem_limit_bytes=VMEM_LIMIT),
        name="dilated_attn",
    )(slopes, za, za, za, za, zh)


def _outproj_kernel(yh_ref, ya_ref, x_ref, w_ref, g_ref, o_ref):
    y = jnp.concatenate([yh_ref[c] for c in range(yh_ref.shape[0])]
                        + [ya_ref[c] for c in range(ya_ref.shape[0])], axis=-1)
    r = x_ref[...] + jnp.dot(y, w_ref[...], preferred_element_type=F32)
    ms = jnp.mean(r * r, axis=-1, keepdims=True)
    o_ref[...] = r * lax.rsqrt(ms + NORM_EPS) * g_ref[...]


def _outproj(yh3, ya3, x2, w_bf16, gain, *, tm):
    m, d = x2.shape
    return pl.pallas_call(
        _outproj_kernel,
        grid=(m // tm,),
        in_specs=[
            pl.BlockSpec((yh3.shape[0], tm, LANES), lambda i: (0, i, 0)),
            pl.BlockSpec((ya3.shape[0], tm, LANES), lambda i: (0, i, 0)),
            pl.BlockSpec((tm, d), lambda i: (i, 0)),
            pl.BlockSpec(w_bf16.shape, lambda i: (0, 0)),
            pl.BlockSpec((1, d), lambda i: (0, 0)),
        ],
        out_specs=pl.BlockSpec((tm, d), lambda i: (i, 0)),
        out_shape=jax.ShapeDtypeStruct((m, d), F32),
        compiler_params=pltpu.CompilerParams(
            dimension_semantics=("parallel",), vmem_limit_bytes=VMEM_LIMIT),
        name="outproj",
    )(yh3, ya3, x2, w_bf16, gain)


def kernel(x, norm_gain, w_in, lb_logits, hgrn_gnorm, w_out, final_gain):
    batch, seq, d = x.shape
    assert d == D_MODEL and w_in.shape == (1, D_MODEL, 4 * D_HGRN + 4 * D_ATTN) and lb_logits.shape[0] == 2
    m = batch * seq
    assert seq % ATTN_TILE == 0

    x2 = x.reshape(m, d)
    w_in_bf16 = w_in[0].astype(BF16)
    gain = norm_gain[0:1]
    zh = _inproj(x2, gain, w_in_bf16, tm=1024, tn=D_HGRN, n_col_tiles=5,
                 col_tile=lambda j: jnp.where(j < 4, j, 7), by_residue=False)
    za = _inproj(x2, gain, w_in_bf16, tm=1024, tn=D_ATTN, n_col_tiles=3,
                 col_tile=lambda j: j + 4, by_residue=True)
    yh3 = _hgrn(zh, lb_logits, hgrn_gnorm[0:1], batch=batch, seq=seq, ts=1024, nh=2)
    slopes = jnp.exp2(-8.0 * jnp.arange(1, ATTN_HEADS + 1, dtype=F32) / ATTN_HEADS)
    ya3 = _attn(za, zh, slopes, batch=batch, seq=seq)
    out = _outproj(yh3, ya3, x2, w_out[0].astype(BF16), final_gain.reshape(1, d), tm=512)
    return out.reshape(batch, seq, d)
```

```python
import functools

import jax
import jax.numpy as jnp
from jax import lax
from jax.experimental import pallas as pl
from jax.experimental.pallas import tpu as pltpu

D_MODEL = 2048
D_HGRN = 1024
HGRN_HEAD = 128
HGRN_HEADS = D_HGRN // HGRN_HEAD
HGRN_CHUNK = 64
D_ATTN = 1024
ATTN_HEAD = 64
ATTN_HEADS = D_ATTN // ATTN_HEAD
DILATED_PATTERNS = ((128, 1), (512, 4), (2048, 16))
NORM_EPS = 1e-6

LANES = 128
SUBLANES = 8
BAND = 128
HEAD_PAIRS = ATTN_HEADS // 2
MASK_VALUE = -1e30
LOG2_E = 1.4426950408889634
RES = 16
ATTN_TILE = BAND * RES

CB_HQ, CB_HF, CB_HI, CB_HG, CB_AG = 0, 8, 16, 24, 32
CB_AQ, CB_AK, CB_AV = 0, 8, 16

VMEM_LIMIT = 56 * 1024 * 1024

F32 = jnp.float32
BF16 = jnp.bfloat16
NT_DIMS = (((1,), (1,)), ((), ()))
TN_DIMS = (((0,), (0,)), ((), ()))


def _silu(v):
    u = 0.5 * v
    return u * jnp.tanh(u) + u


def _inproj_kernel(*refs, by_residue, first_tile_scale):
    n_x = len(refs) - (5 if by_residue else 4)
    x_refs, (g_ref, w_ref, o_ref, h_ref) = refs[:n_x], refs[n_x:n_x + 4]
    tm, d = h_ref.shape
    per_res = tm // RES
    chunk = tm // 4
    j = pl.program_id(1)

    def project(h_rows, row0, scale):
        n_rows = h_rows.shape[0]
        for c in range(o_ref.shape[0] // 2):
            acc = jnp.dot(h_rows, w_ref[:, c * 2 * LANES:(c + 1) * 2 * LANES], preferred_element_type=F32)
            if scale is not None:
                acc = acc * scale
            for half in range(2):
                part = acc[:, half * LANES:(half + 1) * LANES].astype(BF16)
                if by_residue:
                    for k in range(n_rows // per_res):
                        blk = row0 // per_res + k
                        o_ref[2 * c + half, 0, 4 * (blk % 4) + blk // 4] = part[k * per_res:(k + 1) * per_res]
                else:
                    o_ref[2 * c + half, row0:row0 + n_rows] = part

    @pl.when(j == 0)
    def _():
        for c4 in range(4):
            rows = slice(c4 * chunk, (c4 + 1) * chunk)
            if by_residue:
                y_ref = refs[-1]
                for c, x_ref in enumerate(x_refs):
                    y_ref[c, rows, :] = x_ref[pl.ds(c4, chunk, stride=4), :]
                for a in range(4):
                    src = pl.ds(c4 * chunk + a, per_res, stride=4)
                    dst = slice((4 * c4 + a) * per_res, (4 * c4 + a + 1) * per_res)
                    cols = [y_ref[c, src, :] for c in range(n_x)]
                    ssq = cols[0] * cols[0]
                    for xc in cols[1:]:
                        ssq = ssq + xc * xc
                    scale = lax.rsqrt(jnp.sum(ssq, axis=-1, keepdims=True) * (1.0 / d) + NORM_EPS)
                    for c, xc in enumerate(cols):
                        lanes = slice(c * LANES, (c + 1) * LANES)
                        h_ref[dst, lanes] = (xc * scale * g_ref[:, lanes]).astype(BF16)
            else:
                xv = x_refs[0][rows, :]
                ms = jnp.mean(xv * xv, axis=-1, keepdims=True)
                h_ref[rows, :] = (xv * lax.rsqrt(ms + NORM_EPS) * g_ref[...]).astype(BF16)
            project(h_ref[rows, :], c4 * chunk, first_tile_scale if by_residue else None)

    @pl.when(j > 0)
    def _():
        project(h_ref[...], 0, None)


def _inproj(x2, gain, w_bf16, *, tm, tn, n_col_tiles, col_tile, by_residue):
    m, d = x2.shape
    n = n_col_tiles * tn
    if by_residue:
        halves = ATTN_TILE // tm
        out_shape = jax.ShapeDtypeStruct((n // LANES, m // ATTN_TILE, RES, BAND, LANES), BF16)
        out_spec = pl.BlockSpec((tn // LANES, 1, RES, tm // RES, LANES),
                                lambda i, j: (j, i // halves, 0, i % halves, 0))
        x_specs = [pl.BlockSpec((tm, LANES), lambda i, j, c=c: (i, c)) for c in range(d // LANES)]
    else:
        out_shape = jax.ShapeDtypeStruct((n // LANES, m, LANES), BF16)
        out_spec = pl.BlockSpec((tn // LANES, tm, LANES), lambda i, j: (j, i, 0))
        x_specs = [pl.BlockSpec((tm, d), lambda i, j: (i, 0))]
    scratch = [pltpu.VMEM((tm, d), BF16)]
    if by_residue:
        scratch.append(pltpu.VMEM((d // LANES, tm, LANES), F32))
    return pl.pallas_call(
        functools.partial(_inproj_kernel, by_residue=by_residue,
                          first_tile_scale=ATTN_HEAD ** -0.5 * LOG2_E),
        grid=(m // tm, n // tn),
        in_specs=x_specs + [
            pl.BlockSpec((1, d), lambda i, j: (0, 0)),
            pl.BlockSpec((d, tn), lambda i, j: (0, col_tile(j))),
        ],
        out_specs=out_spec,
        out_shape=out_shape,
        scratch_shapes=scratch,
        compiler_params=pltpu.CompilerParams(
            dimension_semantics=("parallel", "arbitrary"), vmem_limit_bytes=VMEM_LIMIT),
        name="inproj_attn" if by_residue else "inproj_hgrn",
    )(*([x2] * len(x_specs)), gain, w_bf16)


def _hgrn_kernel(q_ref, f_ref, i_ref, g_ref, lbl_ref, gn_ref, o_ref, st_ref):
    nh, ts, _ = q_ref.shape
    heads = range(nh)
    c_len = HGRN_CHUNK
    chunks = [slice(c * c_len, (c + 1) * c_len) for c in range(ts // c_len)]
    row = lax.broadcasted_iota(jnp.int32, (c_len, c_len), 0)
    col = lax.broadcasted_iota(jnp.int32, (c_len, c_len), 1)
    causal = col <= row
    tril = causal.astype(BF16)

    @pl.when(pl.program_id(2) == 0)
    def _():
        st_ref[...] = jnp.zeros_like(st_ref)

    lbl = lbl_ref[...]
    e = jnp.exp(lbl - jnp.max(lbl, axis=0, keepdims=True))
    lb_all = e[0:1, :] / jnp.sum(e, axis=0, keepdims=True)
    lb = [lb_all[:, h * LANES:(h + 1) * LANES] for h in heads]

    q = [_silu(q_ref[h].astype(F32)) for h in heads]
    half_range = [0.5 * (1.0 - lb[h]) for h in heads]
    mid = [lb[h] + half_range[h] for h in heads]
    f = [mid[h] + half_range[h] * jnp.tanh(0.5 * f_ref[h].astype(F32)) for h in heads]
    k = [1.0 - f[h] for h in heads]
    log_f = [jnp.log2(f[h]) for h in heads]
    hilo = []
    for h in heads:
        hi = log_f[h].astype(BF16)
        lo = (log_f[h] - hi.astype(F32)).astype(BF16)
        hilo.append(jnp.concatenate([hi, lo], axis=-1))
    cs = [jnp.concatenate([jnp.dot(tril, hilo[h][sl], preferred_element_type=F32) for sl in chunks], axis=0)
          for h in heads]
    b_cum = [cs[h][:, :LANES] + cs[h][:, LANES:] for h in heads]
    decay = [[jnp.exp2(b_cum[h][sl.stop - 1:sl.stop, :]) for sl in chunks] for h in heads]
    q_dec = [(q[h] * jnp.exp2(b_cum[h])).astype(BF16) for h in heads]
    k_dec_f = [k[h] * jnp.exp2(-b_cum[h]) for h in heads]
    k_dec = [k_dec_f[h].astype(BF16) for h in heads]
    k_end = [jnp.concatenate([k_dec_f[h][sl] * decay[h][c] for c, sl in enumerate(chunks)], axis=0).astype(BF16)
             for h in heads]
    v = [i_ref[h] for h in heads]
    gn = gn_ref[...]

    kv_t = [[lax.dot_general(v[h][sl], k_end[h][sl], TN_DIMS, preferred_element_type=F32) for sl in chunks]
            for h in heads]
    a = [[lax.dot_general(q_dec[h][sl], k_dec[h][sl], NT_DIMS, preferred_element_type=F32) for sl in chunks]
         for h in heads]
    a = [[jnp.where(causal, a_c, 0.0).astype(BF16) for a_c in a[h]] for h in heads]
    entering = [[] for _ in heads]
    for h in heads:
        st = st_ref[h]
        for c in range(len(chunks)):
            entering[h].append(st.astype(BF16))
            st = st * decay[h][c] + kv_t[h][c]
        st_ref[h] = st
    for h in heads:
        o = jnp.concatenate(
            [jnp.dot(a[h][c], v[h][sl], preferred_element_type=F32)
             + lax.dot_general(q_dec[h][sl], entering[h][c], NT_DIMS, preferred_element_type=F32)
             for c, sl in enumerate(chunks)], axis=0)
        o = o * lax.rsqrt(jnp.mean(o * o, axis=-1, keepdims=True) + NORM_EPS) * gn
        o_ref[h] = (o * _silu(g_ref[h].astype(F32))).astype(BF16)


def _hgrn(zh, lb_logits, gnorm, *, batch, seq, ts, nh):
    nst = seq // ts

    def zspec(cb):
        return pl.BlockSpec((nh, ts, LANES), lambda b, h, s: (cb // nh + h, b * nst + s, 0))

    return pl.pallas_call(
        _hgrn_kernel,
        grid=(batch, HGRN_HEADS // nh, nst),
        in_specs=[
            zspec(CB_HQ), zspec(CB_HF), zspec(CB_HI), zspec(CB_HG),
            pl.BlockSpec((lb_logits.shape[0], nh * LANES), lambda b, h, s: (0, h)),
            pl.BlockSpec((1, LANES), lambda b, h, s: (0, 0)),
        ],
        out_specs=pl.BlockSpec((nh, ts, LANES), lambda b, h, s: (h, b * nst + s, 0)),
        out_shape=jax.ShapeDtypeStruct((HGRN_HEADS, batch * seq, LANES), BF16),
        scratch_shapes=[pltpu.VMEM((nh, HGRN_HEAD, HGRN_HEAD), F32)],
        compiler_params=pltpu.CompilerParams(
            dimension_semantics=("parallel", "parallel", "arbitrary"), vmem_limit_bytes=VMEM_LIMIT),
        name="hgrn2",
    )(zh, zh, zh, zh, lb_logits, gnorm)


def _unit_positions(p):
    rho = lax.broadcasted_iota(jnp.int32, (BAND, 2 * BAND), 0)
    kap = lax.broadcasted_iota(jnp.int32, (BAND, 2 * BAND), 1)
    if p == 2:
        return BAND + rho, kap
    if p == 1:
        q_pos = BAND + 4 * (rho & 31) + (rho >> 5)
        k_pos = (kap >> 7) * BAND + 4 * (kap & 31) + ((kap >> 5) & 3)
        return q_pos, k_pos
    q_pos = BAND + RES * (rho & 7) + (rho >> 3)
    k_pos = RES * (kap & 15) + (kap >> 4)
    return q_pos, k_pos


def _scores(q2, kb):
    return lax.dot_general(q2, kb, NT_DIMS, preferred_element_type=F32)


def _probs(s, bias0, bias1, head0):
    s0 = s[:BAND] + bias0
    m0 = jnp.max(s0, axis=-1, keepdims=True)
    p0 = jnp.exp2(s0 - m0).astype(BF16)
    s1 = s[BAND:] + bias1
    m1 = jnp.max(s1, axis=-1, keepdims=True)
    p1 = jnp.exp2(s1 - m1).astype(BF16)
    return jnp.concatenate([p0, p1], axis=1), jnp.where(head0, m0, m1)


def _attn_kernel(slopes_ref, q_ref, k_ref, v_ref, kp_ref, g_ref, o_ref,
                 bias_ref, qm_ref, vm_ref, qf_ref, kf_ref, vf_ref, acc_ref, mx_ref, den_ref, nat_ref):
    hp = pl.program_id(1)
    tile = pl.program_id(2)
    first_tile = jnp.where(tile == 0, 1, 0)
    slot = lax.bitwise_and(tile, 1)
    lane = lax.broadcasted_iota(jnp.int32, (1, LANES), 1)
    head0 = lane < ATTN_HEAD
    prev_rows = SUBLANES

    @pl.when(tile == 0)
    def _():
        vm_ref[0, 1] = jnp.zeros((RES, BAND, LANES), BF16)
        vm_ref[1, 1] = jnp.zeros((RES, BAND, LANES), BF16)
        for p, (_, dil) in enumerate(DILATED_PATTERNS):
            q_pos, k_pos = _unit_positions(p)
            delta = q_pos - k_pos
            in_band = (delta >= 0) & (delta <= BAND)
            not_prev = k_pos >= BAND
            delta_f = delta.astype(F32)
            for hh in range(2):
                slope = slopes_ref[2 * hp + hh]
                bias = -(slope * (float(dil) * LOG2_E)) * delta_f
                bias_ref[p, hh, 0] = jnp.where(in_band, bias, MASK_VALUE)
                bias_ref[p, hh, 1] = jnp.where(in_band & not_prev, bias, MASK_VALUE)

    def split_heads(x):
        zero = jnp.zeros_like(x)
        return jnp.where(head0, x, zero), jnp.where(head0, zero, x)

    qm_ref[0], qm_ref[1] = split_heads(q_ref[0, 0])
    vm_ref[0, slot], vm_ref[1, slot] = split_heads(v_ref[0, 0])

    def unpack_for_d1():
        qf_ref[...] = q_ref[0, 0].astype(F32)
        packed_rows = 2 * SUBLANES
        v_tail = (vm_ref[0, 1 - slot, :, BAND - packed_rows:, :].astype(F32)
                  + vm_ref[1, 1 - slot, :, BAND - packed_rows:, :].astype(F32))
        vf_ref[:, :prev_rows, :] = v_tail[:, packed_rows - prev_rows:, :]
        vf_ref[:, prev_rows:, :] = v_ref[0, 0].astype(F32)
        k_tail = kp_ref[0, 0, :, BAND - packed_rows:, :].astype(F32)
        kf_ref[:, :prev_rows, :] = k_tail[:, packed_rows - prev_rows:, :]
        kf_ref[:, prev_rows:, :] = k_ref[0, 0].astype(F32)

    key_row = lax.broadcasted_iota(jnp.int32, (4 * BAND, LANES), 0)
    key_lane = lax.broadcasted_iota(jnp.int32, (4 * BAND, LANES), 1)
    den_cols = ((key_row < 2 * BAND) == (key_lane < ATTN_HEAD)).astype(BF16)

    def pv(pc, v2):
        return jnp.dot(pc, jnp.concatenate([v2, den_cols], axis=1), preferred_element_type=F32)

    rows4 = BAND // 4

    def unit_d16(r):
        def q2():
            return jnp.concatenate([qm_ref[0, r], qm_ref[1, r]], axis=0)

        def kb():
            return jnp.concatenate([kp_ref[0, 0, r], k_ref[0, 0, r]], axis=0)

        def v2():
            return jnp.concatenate([vm_ref[0, 1 - slot, r], vm_ref[0, slot, r],
                                    vm_ref[1, 1 - slot, r], vm_ref[1, slot, r]], axis=0)

        def store(acc, den, m_full):
            acc_ref[1, r] = acc
            den_ref[1, r] = den
            mx_ref[1, r] = m_full
        return 2, first_tile, q2, kb, v2, store

    def unit_d4(c, n):
        cur = slice(n * rows4, (n + 1) * rows4)
        prv = slice((n - 1) * rows4, n * rows4) if n else slice(BAND - rows4, BAND)
        slabs = [4 * a + c for a in range(4)]

        def q2():
            return jnp.concatenate([qm_ref[hh, sb, cur] for hh in range(2) for sb in slabs], axis=0)

        def kb():
            src = k_ref if n else kp_ref
            return jnp.concatenate([src[0, 0, sb, prv] for sb in slabs] + [k_ref[0, 0, sb, cur] for sb in slabs], axis=0)

        def v2():
            prev_slot = slot if n else 1 - slot
            return jnp.concatenate(
                [vm_ref[hh, prev_slot, sb, prv] if part == 0 else vm_ref[hh, slot, sb, cur]
                 for hh in range(2) for part in range(2) for sb in slabs], axis=0)

        def store(acc, den, m_full):
            for a, sb in enumerate(slabs):
                piece = slice(a * rows4, (a + 1) * rows4)
                acc_ref[0, sb, cur] = acc[piece]
                den_ref[0, sb, cur] = den[piece]
                mx_ref[0, sb, cur] = m_full[piece]
        return 1, (first_tile if n == 0 else 0), q2, kb, v2, store

    def unit_d1(n):
        q_rows = slice(n * SUBLANES, (n + 1) * SUBLANES)
        k_rows = slice(n * SUBLANES, (n + 2) * SUBLANES)
        tok = n * BAND

        def q2():
            return jnp.concatenate(split_heads(qf_ref[:, q_rows, :].reshape(BAND, LANES)), axis=0).astype(BF16)

        def kb():
            return kf_ref[:, k_rows, :].reshape(2 * BAND, LANES).astype(BF16)

        def v2():
            return jnp.concatenate(split_heads(vf_ref[:, k_rows, :].reshape(2 * BAND, LANES)), axis=0).astype(BF16)

        def store(acc, den, m_full):
            others = [tuple(ref[p, :, q_rows, :].reshape(BAND, LANES) for ref in (acc_ref, den_ref, mx_ref))
                      for p in range(2)]
            m_all = jnp.maximum(jnp.maximum(m_full, others[0][2]), others[1][2])
            w = jnp.exp2(m_full - m_all)
            num = w * acc
            total = w * den
            for acc_p, den_p, m_p in others:
                w = jnp.exp2(m_p - m_all)
                num = num + w * acc_p
                total = total + w * den_p
            y = (num / total).reshape(RES, SUBLANES, LANES)
            for r in range(RES):
                nat_ref[pl.ds(tok + r, SUBLANES, stride=RES), :] = y[r]
            rows = slice(tok, tok + BAND)
            gate = _silu(g_ref[0, rows, :].astype(F32))
            o_ref[0, rows, :] = (nat_ref[rows, :] * gate).astype(BF16)
        return 0, (first_tile if n == 0 else 0), q2, kb, v2, store

    units = [unit_d16(r) for r in range(RES)]
    d1_ready = []
    for n4 in range(4):
        for c in range(4):
            units.append(unit_d4(c, n4))
            if d1_ready:
                units.append(unit_d1(d1_ready.pop(0)))
        d1_ready.extend(range(4 * n4, 4 * n4 + 4))
    units.extend(unit_d1(n) for n in d1_ready)

    s, pm = {}, {}
    sm_lag, pv_lag = 1, 3
    for i in range(len(units) + pv_lag):
        if i == 4:
            unpack_for_d1()
        if i < len(units):
            s[i] = _scores(units[i][2](), units[i][3]())
        if 0 <= i - sm_lag < len(units):
            p, var = units[i - sm_lag][:2]
            pm[i - sm_lag] = _probs(s.pop(i - sm_lag), bias_ref[p, 0, var], bias_ref[p, 1, var], head0)
        if 0 <= i - pv_lag < len(units):
            pc, m_full = pm.pop(i - pv_lag)
            od = pv(pc, units[i - pv_lag][4]())
            units[i - pv_lag][5](od[:, :LANES], od[:, LANES:], m_full)


def _attn(za, zh, slopes, *, batch, seq):
    nt_ = seq // ATTN_TILE
    n_pat = len(DILATED_PATTERNS)
    blk = (1, 1, RES, BAND, LANES)

    def zspec(cb):
        return pl.BlockSpec(blk, lambda b, h, t, sl: (cb + h, b * nt_ + t, 0, 0, 0))

    prev_keys = pl.BlockSpec(blk, lambda b, h, t, sl: (CB_AK + h, b * nt_ + jnp.maximum(t - 1, 0), 0, 0, 0))
    gate = pl.BlockSpec((1, ATTN_TILE, LANES), lambda b, h, t, sl: (CB_AG + h, b * nt_ + t, 0))
    grid_spec = pltpu.PrefetchScalarGridSpec(
        num_scalar_prefetch=1,
        grid=(batch, HEAD_PAIRS, nt_),
        in_specs=[zspec(CB_AQ), zspec(CB_AK), zspec(CB_AV), prev_keys, gate],
        out_specs=pl.BlockSpec((1, ATTN_TILE, LANES), lambda b, h, t, sl: (h, b * nt_ + t, 0)),
        scratch_shapes=[
            pltpu.VMEM((n_pat, 2, 2, BAND, 2 * BAND), F32),
            pltpu.VMEM((2, RES, BAND, LANES), BF16),
            pltpu.VMEM((2, 2, RES, BAND, LANES), BF16),
            pltpu.VMEM((RES, BAND, LANES), F32),
            pltpu.VMEM((RES, SUBLANES + BAND, LANES), F32),
            pltpu.VMEM((RES, SUBLANES + BAND, LANES), F32),
            pltpu.VMEM((n_pat - 1, RES, BAND, LANES), F32),
            pltpu.VMEM((n_pat - 1, RES, BAND, LANES), F32),
            pltpu.VMEM((n_pat - 1, RES, BAND, LANES), F32),
            pltpu.VMEM((ATTN_TILE, LANES), F32),
        ],
    )
    return pl.pallas_call(
        _attn_kernel,
        grid_spec=grid_spec,
        out_shape=jax.ShapeDtypeStruct((HEAD_PAIRS, batch * seq, LANES), BF16),
        compiler_params=pltpu.CompilerParams(
            dimension_semantics=("parallel", "parallel", "arbitrary"), vmem_limit_bytes=VMEM_LIMIT),
        name="dilated_attn",
    )(slopes, za, za, za, za, zh)


def _outproj_kernel(yh_ref, ya_ref, x_ref, w_ref, g_ref, o_ref):
    y = jnp.concatenate([yh_ref[c] for c in range(yh_ref.shape[0])]
                        + [ya_ref[c] for c in range(ya_ref.shape[0])], axis=-1)
    r = x_ref[...] + jnp.dot(y, w_ref[...], preferred_element_type=F32)
    ms = jnp.mean(r * r, axis=-1, keepdims=True)
    o_ref[...] = r * lax.rsqrt(ms + NORM_EPS) * g_ref[...]


def _outproj(yh3, ya3, x2, w_bf16, gain, *, tm):
    m, d = x2.shape
    return pl.pallas_call(
        _outproj_kernel,
        grid=(m // tm,),
        in_specs=[
            pl.BlockSpec((yh3.shape[0], tm, LANES), lambda i: (0, i, 0)),
            pl.BlockSpec((ya3.shape[0], tm, LANES), lambda i: (0, i, 0)),
            pl.BlockSpec((tm, d), lambda i: (i, 0)),
            pl.BlockSpec(w_bf16.shape, lambda i: (0, 0)),
            pl.BlockSpec((1, d), lambda i: (0, 0)),
        ],
        out_specs=pl.BlockSpec((tm, d), lambda i: (i, 0)),
        out_shape=jax.ShapeDtypeStruct((m, d), F32),
        compiler_params=pltpu.CompilerParams(
            dimension_semantics=("parallel",), vmem_limit_bytes=VMEM_LIMIT),
        name="outproj",
    )(yh3, ya3, x2, w_bf16, gain)


def kernel(x, norm_gain, w_in, lb_logits, hgrn_gnorm, w_out, final_gain):
    batch, seq, d = x.shape
    assert d == D_MODEL and w_in.shape == (1, D_MODEL, 4 * D_HGRN + 4 * D_ATTN) and lb_logits.shape[0] == 2
    m = batch * seq
    assert seq % ATTN_TILE == 0

    x2 = x.reshape(m, d)
    w_in_bf16 = w_in[0].astype(BF16)
    gain = norm_gain[0:1]
    zh = _inproj(x2, gain, w_in_bf16, tm=1024, tn=D_HGRN, n_col_tiles=5,
                 col_tile=lambda j: jnp.where(j < 4, j, 7), by_residue=False)
    za = _inproj(x2, gain, w_in_bf16, tm=1024, tn=D_ATTN, n_col_tiles=3,
                 col_tile=lambda j: j + 4, by_residue=True)
    yh3 = _hgrn(zh, lb_logits, hgrn_gnorm[0:1], batch=batch, seq=seq, ts=1024, nh=2)
    slopes = jnp.exp2(-8.0 * jnp.arange(1, ATTN_HEADS + 1, dtype=F32) / ATTN_HEADS)
    ya3 = _attn(za, zh, slopes, batch=batch, seq=seq)
    out = _outproj(yh3, ya3, x2, w_out[0].astype(BF16), final_gain.reshape(1, d), tm=512)
    return out.reshape(batch, seq, d)
```

```python
import functools

import jax
import jax.numpy as jnp
from jax import lax
from jax.experimental import pallas as pl
from jax.experimental.pallas import tpu as pltpu

D_MODEL = 2048
D_HGRN = 1024
HGRN_HEAD = 128
HGRN_HEADS = D_HGRN // HGRN_HEAD
HGRN_CHUNK = 64
D_ATTN = 1024
ATTN_HEAD = 64
ATTN_HEADS = D_ATTN // ATTN_HEAD
DILATED_PATTERNS = ((128, 1), (512, 4), (2048, 16))
NORM_EPS = 1e-6

LANES = 128
SUBLANES = 8
BAND = 128
HEAD_PAIRS = ATTN_HEADS // 2
MASK_VALUE = -1e30
LOG2_E = 1.4426950408889634
RES = 16
ATTN_TILE = BAND * RES

CB_HQ, CB_HF, CB_HI, CB_HG, CB_AG = 0, 8, 16, 24, 32
CB_AQ, CB_AK, CB_AV = 0, 8, 16

VMEM_LIMIT = 56 * 1024 * 1024

F32 = jnp.float32
BF16 = jnp.bfloat16
NT_DIMS = (((1,), (1,)), ((), ()))
TN_DIMS = (((0,), (0,)), ((), ()))


def _silu(v):
    u = 0.5 * v
    return u * jnp.tanh(u) + u


def _interleave(pieces, phases, advance_after):
    for _ in range(advance_after.get(-1, 0)):
        next(phases, None)
    for k, piece in enumerate(pieces):
        piece()
        for _ in range(advance_after.get(k, 0)):
            next(phases, None)
    for _ in phases:
        pass


def _inproj_kernel(*refs, by_residue, n_x, q_cols, q_scale, hgrn_steps_per_seq):
    x_refs, rest = refs[:n_x], refs[n_x:]
    if by_residue:
        g_ref, w_ref, zq_ref, zf_ref, zi_ref, zg_ref, lbl_ref, gn_ref, o_ref, yo_ref, h_ref, y_ref, st_ref = rest
    else:
        g_ref, w_ref, o_ref, h_ref = rest
    tm, d = h_ref.shape
    tn = w_ref.shape[1]
    per_res = tm // RES
    chunk = tm // 4
    i = pl.program_id(0)
    j = pl.program_id(1)
    sub = 2 * LANES

    def project(c, h_rows, row0, scale):
        n_rows = h_rows.shape[0]
        acc = jnp.dot(h_rows, w_ref[:, c * sub:(c + 1) * sub], preferred_element_type=F32)
        if scale is not None:
            acc = acc * scale
        for half in range(2):
            part = acc[:, half * LANES:(half + 1) * LANES].astype(BF16)
            if by_residue:
                for k in range(n_rows // per_res):
                    blk = row0 // per_res + k
                    o_ref[2 * c + half, 0, 4 * (blk % 4) + blk // 4] = part[k * per_res:(k + 1) * per_res]
            else:
                o_ref[2 * c + half, row0:row0 + n_rows] = part

    def normalise_quarter(c4):
        rows = slice(c4 * chunk, (c4 + 1) * chunk)
        if by_residue:
            for c, x_ref in enumerate(x_refs):
                y_ref[c, rows, :] = x_ref[pl.ds(c4, chunk, stride=4), :]
            for a in range(4):
                src = pl.ds(c4 * chunk + a, per_res, stride=4)
                dst = slice((4 * c4 + a) * per_res, (4 * c4 + a + 1) * per_res)
                cols = [y_ref[c, src, :] for c in range(n_x)]
                ssq = cols[0] * cols[0]
                for xc in cols[1:]:
                    ssq = ssq + xc * xc
                scale = lax.rsqrt(jnp.sum(ssq, axis=-1, keepdims=True) * (1.0 / d) + NORM_EPS)
                for c, xc in enumerate(cols):
                    lanes = slice(c * LANES, (c + 1) * LANES)
                    h_ref[dst, lanes] = (xc * scale * g_ref[:, lanes]).astype(BF16)
        else:
            xv = x_refs[0][rows, :]
            ms = jnp.mean(xv * xv, axis=-1, keepdims=True)
            h_ref[rows, :] = (xv * lax.rsqrt(ms + NORM_EPS) * g_ref[...]).astype(BF16)

    def hgrn():
        step = i * pl.num_programs(1) + j
        first = lax.rem(step, hgrn_steps_per_seq) == 0
        return _hgrn_phases(zq_ref, zf_ref, zi_ref, zg_ref, lbl_ref, gn_ref, yo_ref, st_ref, first)

    def run(pieces):
        if by_residue:
            n = len(pieces)
            _interleave(pieces, hgrn(), {-1: 1, n // 3 - 1: 2, 2 * n // 3 - 1: 2, n - 1: 2})
        else:
            for piece in pieces:
                piece()

    @pl.when(j == 0)
    def _():
        pieces = []
        for c4 in range(4):
            rows = slice(c4 * chunk, (c4 + 1) * chunk)
            for c in range(tn // sub):
                scale = q_scale if c * sub < q_cols else None

                def piece(c4=c4, c=c, rows=rows, scale=scale):
                    if c == 0:
                        normalise_quarter(c4)
                    project(c, h_ref[rows, :], c4 * chunk, scale)
                pieces.append(piece)
        run(pieces)

    @pl.when(j > 0)
    def _():
        pieces = []
        for c in range(tn // sub):
            n_q_tiles = -(-(q_cols - c * sub) // tn) if q_cols > c * sub else 0
            scale = jnp.where(j < n_q_tiles, q_scale, 1.0) if n_q_tiles > 1 else None
            pieces.append(lambda c=c, scale=scale: project(c, h_ref[...], 0, scale))
        run(pieces)


def _inproj(x2, gain, w_bf16, *, tm, tn, n_col_tiles, col_tile, by_residue, hgrn=None):
    m, d = x2.shape
    n = n_col_tiles * tn
    grid = (m // tm, n_col_tiles)
    scratch = [pltpu.VMEM((tm, d), BF16)]
    extra_in, extra_specs, out_shapes, out_specs = [], [], [], []
    if by_residue:
        halves = ATTN_TILE // tm
        out_shapes.append(jax.ShapeDtypeStruct((n // LANES, m // ATTN_TILE, RES, BAND, LANES), BF16))
        out_specs.append(pl.BlockSpec((tn // LANES, 1, RES, tm // RES, LANES),
                                      lambda i, j: (j, i // halves, 0, i % halves, 0)))
        x_specs = [pl.BlockSpec((tm, LANES), lambda i, j, c=c: (i, c)) for c in range(d // LANES)]
        scratch.append(pltpu.VMEM((d // LANES, tm, LANES), F32))

        zh, lb_logits, gnorm, batch, seq, ts = hgrn
        nh = 2
        pairs, seq_tiles = HGRN_HEADS // nh, seq // ts
        assert grid[0] * grid[1] == batch * pairs * seq_tiles

        def where(i, j):
            step = i * n_col_tiles + j
            return (step // seq_tiles) % pairs, (step // (pairs * seq_tiles)) * seq_tiles + step % seq_tiles

        def zspec(cb):
            return pl.BlockSpec((nh, ts, LANES), lambda i, j: (cb // nh + where(i, j)[0], where(i, j)[1], 0))

        extra_in = [zh, zh, zh, zh, lb_logits, gnorm]
        extra_specs = [zspec(CB_HQ), zspec(CB_HF), zspec(CB_HI), zspec(CB_HG),
                       pl.BlockSpec((lb_logits.shape[0], nh * LANES), lambda i, j: (0, where(i, j)[0])),
                       pl.BlockSpec((1, LANES), lambda i, j: (0, 0))]
        out_shapes.append(jax.ShapeDtypeStruct((HGRN_HEADS, batch * seq, LANES), BF16))
        out_specs.append(pl.BlockSpec((nh, ts, LANES), lambda i, j: (where(i, j)[0], where(i, j)[1], 0)))
        scratch.append(pltpu.VMEM((nh, HGRN_HEAD, HGRN_HEAD), F32))
        steps_per_seq = seq_tiles
    else:
        out_shapes.append(jax.ShapeDtypeStruct((n // LANES, m, LANES), BF16))
        out_specs.append(pl.BlockSpec((tn // LANES, tm, LANES), lambda i, j: (j, i, 0)))
        x_specs = [pl.BlockSpec((tm, d), lambda i, j: (i, 0))]
        steps_per_seq = None
    outs = pl.pallas_call(
        functools.partial(_inproj_kernel, by_residue=by_residue, n_x=len(x_specs),
                          q_cols=D_ATTN if by_residue else 0, q_scale=ATTN_HEAD ** -0.5 * LOG2_E,
                          hgrn_steps_per_seq=steps_per_seq),
        grid=grid,
        in_specs=x_specs + [
            pl.BlockSpec((1, d), lambda i, j: (0, 0)),
            pl.BlockSpec((d, tn), lambda i, j: (0, col_tile(j))),
        ] + extra_specs,
        out_specs=out_specs,
        out_shape=out_shapes,
        scratch_shapes=scratch,
        compiler_params=pltpu.CompilerParams(
            dimension_semantics=("arbitrary", "arbitrary") if by_residue else ("parallel", "arbitrary"),
            vmem_limit_bytes=VMEM_LIMIT),
        name="inproj_attn_hgrn2" if by_residue else "inproj_token",
    )(*([x2] * len(x_specs)), gain, w_bf16, *extra_in)
    return outs if by_residue else outs[0]


def _hgrn_phases(q_ref, f_ref, i_ref, g_ref, lbl_ref, gn_ref, o_ref, st_ref, first_of_sequence):
    nh, ts, _ = q_ref.shape
    heads = range(nh)
    c_len = HGRN_CHUNK
    chunks = [slice(c * c_len, (c + 1) * c_len) for c in range(ts // c_len)]
    row = lax.broadcasted_iota(jnp.int32, (c_len, c_len), 0)
    col = lax.broadcasted_iota(jnp.int32, (c_len, c_len), 1)
    causal = col <= row
    tril = causal.astype(BF16)

    lbl = lbl_ref[...]
    e = jnp.exp(lbl - jnp.max(lbl, axis=0, keepdims=True))
    lb_all = e[0:1, :] / jnp.sum(e, axis=0, keepdims=True)
    lb = [lb_all[:, h * LANES:(h + 1) * LANES] for h in heads]

    q = [_silu(q_ref[h].astype(F32)) for h in heads]
    half_range = [0.5 * (1.0 - lb[h]) for h in heads]
    mid = [lb[h] + half_range[h] for h in heads]
    f = [mid[h] + half_range[h] * jnp.tanh(0.5 * f_ref[h].astype(F32)) for h in heads]
    k = [1.0 - f[h] for h in heads]
    log_f = [jnp.log2(f[h]) for h in heads]
    hilo = []
    for h in heads:
        hi = log_f[h].astype(BF16)
        lo = (log_f[h] - hi.astype(F32)).astype(BF16)
        hilo.append(jnp.concatenate([hi, lo], axis=-1))
    yield
    cs = [jnp.concatenate([jnp.dot(tril, hilo[h][sl], preferred_element_type=F32) for sl in chunks], axis=0)
          for h in heads]
    yield
    b_cum =[cs[h][:, :LANES] + cs[h][:, LANES:] for h in heads]
    decay = [[jnp.exp2(b_cum[h][sl.stop - 1:sl.stop, :]) for sl in chunks] for h in heads]
    q_dec = [(q[h] * jnp.exp2(b_cum[h])).astype(BF16) for h in heads]
    k_dec_f = [k[h] * jnp.exp2(-b_cum[h]) for h in heads]
    k_dec = [k_dec_f[h].astype(BF16) for h in heads]
    k_end = [jnp.concatenate([k_dec_f[h][sl] * decay[h][c] for c, sl in enumerate(chunks)], axis=0).astype(BF16)
             for h in heads]
    v = [i_ref[h] for h in heads]
    yield
    kv_t = [[lax.dot_general(v[h][sl], k_end[h][sl], TN_DIMS, preferred_element_type=F32) for sl in chunks]
            for h in heads]
    a = [[lax.dot_general(q_dec[h][sl], k_dec[h][sl], NT_DIMS, preferred_element_type=F32) for sl in chunks]
         for h in heads]
    yield
    a = [[jnp.where(causal, a_c, 0.0).astype(BF16) for a_c in a[h]] for h in heads]
    entering = [[] for _ in heads]
    for h in heads:
        st = jnp.where(first_of_sequence, 0.0, st_ref[h])
        for c in range(len(chunks)):
            entering[h].append(st.astype(BF16))
            st = st * decay[h][c] + kv_t[h][c]
        st_ref[h] = st
    yield
    o = [jnp.concatenate(
        [jnp.dot(a[h][c], v[h][sl], preferred_element_type=F32)
         + lax.dot_general(q_dec[h][sl], entering[h][c], NT_DIMS, preferred_element_type=F32)
         for c, sl in enumerate(chunks)], axis=0) for h in heads]
    yield
    gn = gn_ref[...]
    for h in heads:
        o_h = o[h] * lax.rsqrt(jnp.mean(o[h] * o[h], axis=-1, keepdims=True) + NORM_EPS) * gn
        o_ref[h] = (o_h * _silu(g_ref[h].astype(F32))).astype(BF16)


def _unit_positions(p):
    rho = lax.broadcasted_iota(jnp.int32, (BAND, 2 * BAND), 0)
    kap = lax.broadcasted_iota(jnp.int32, (BAND, 2 * BAND), 1)
    if p == 2:
        return BAND + rho, kap
    if p == 1:
        q_pos = BAND + 4 * (rho & 31) + (rho >> 5)
        k_pos = (kap >> 7) * BAND + 4 * (kap & 31) + ((kap >> 5) & 3)
        return q_pos, k_pos
    q_pos = BAND + RES * (rho & 7) + (rho >> 3)
    k_pos = RES * (kap & 15) + (kap >> 4)
    return q_pos, k_pos


def _scores(q2, kb):
    return lax.dot_general(q2, kb, NT_DIMS, preferred_element_type=F32)


def _probs(s, bias0, bias1, head0):
    s0 = s[:BAND] + bias0
    m0 = jnp.max(s0, axis=-1, keepdims=True)
    p0 = jnp.exp2(s0 - m0).astype(BF16)
    s1 = s[BAND:] + bias1
    m1 = jnp.max(s1, axis=-1, keepdims=True)
    p1 = jnp.exp2(s1 - m1).astype(BF16)
    return jnp.concatenate([p0, p1], axis=1), jnp.where(head0, m0, m1)


def _attn_kernel(slopes_ref, q_ref, k_ref, v_ref, kp_ref, g_ref, o_ref,
                 bias_ref, qm_ref, vm_ref, qf_ref, kf_ref, vf_ref, acc_ref, mx_ref, den_ref, nat_ref):
    hp = pl.program_id(1)
    tile = pl.program_id(2)
    first_tile = jnp.where(tile == 0, 1, 0)
    slot = lax.bitwise_and(tile, 1)
    lane = lax.broadcasted_iota(jnp.int32, (1, LANES), 1)
    head0 = lane < ATTN_HEAD
    prev_rows = SUBLANES

    @pl.when(tile == 0)
    def _():
        vm_ref[0, 1] = jnp.zeros((RES, BAND, LANES), BF16)
        vm_ref[1, 1] = jnp.zeros((RES, BAND, LANES), BF16)
        for p, (_, dil) in enumerate(DILATED_PATTERNS):
            q_pos, k_pos = _unit_positions(p)
            delta = q_pos - k_pos
            in_band = (delta >= 0) & (delta <= BAND)
            not_prev = k_pos >= BAND
            delta_f = delta.astype(F32)
            for hh in range(2):
                slope = slopes_ref[2 * hp + hh]
                bias = -(slope * (float(dil) * LOG2_E)) * delta_f
                bias_ref[p, hh, 0] = jnp.where(in_band, bias, MASK_VALUE)
                bias_ref[p, hh, 1] = jnp.where(in_band & not_prev, bias, MASK_VALUE)

    def split_heads(x):
        zero = jnp.zeros_like(x)
        return jnp.where(head0, x, zero), jnp.where(head0, zero, x)

    qm_ref[0], qm_ref[1] = split_heads(q_ref[0, 0])
    vm_ref[0, slot], vm_ref[1, slot] = split_heads(v_ref[0, 0])

    def unpack_for_d1():
        qf_ref[...] = q_ref[0, 0].astype(F32)
        packed_rows = 2 * SUBLANES
        v_tail = (vm_ref[0, 1 - slot, :, BAND - packed_rows:, :].astype(F32)
                  + vm_ref[1, 1 - slot, :, BAND - packed_rows:, :].astype(F32))
        vf_ref[:, :prev_rows, :] = v_tail[:, packed_rows - prev_rows:, :]
        vf_ref[:, prev_rows:, :] = v_ref[0, 0].astype(F32)
        k_tail = kp_ref[0, 0, :, BAND - packed_rows:, :].astype(F32)
        kf_ref[:, :prev_rows, :] = k_tail[:, packed_rows - prev_rows:, :]
        kf_ref[:, prev_rows:, :] = k_ref[0, 0].astype(F32)

    key_row = lax.broadcasted_iota(jnp.int32, (4 * BAND, LANES), 0)
    key_lane = lax.broadcasted_iota(jnp.int32, (4 * BAND, LANES), 1)
    den_cols = ((key_row < 2 * BAND) == (key_lane < ATTN_HEAD)).astype(BF16)

    def pv(pc, v2):
        return jnp.dot(pc, jnp.concatenate([v2, den_cols], axis=1), preferred_element_type=F32)

    rows4 = BAND // 4

    def unit_d16(r):
        def q2():
            return jnp.concatenate([qm_ref[0, r], qm_ref[1, r]], axis=0)

        def kb():
            return jnp.concatenate([kp_ref[0, 0, r], k_ref[0, 0, r]], axis=0)

        def v2():
            return jnp.concatenate([vm_ref[0, 1 - slot, r], vm_ref[0, slot, r],
                                    vm_ref[1, 1 - slot, r], vm_ref[1, slot, r]], axis=0)

        def store(acc, den, m_full):
            acc_ref[1, r] = acc
            den_ref[1, r] = den
            mx_ref[1, r] = m_full
        return 2, first_tile, q2, kb, v2, store

    def unit_d4(c, n):
        cur = slice(n * rows4, (n + 1) * rows4)
        prv = slice((n - 1) * rows4, n * rows4) if n else slice(BAND - rows4, BAND)
        slabs = [4 * a + c for a in range(4)]

        def q2():
            return jnp.concatenate([qm_ref[hh, sb, cur] for hh in range(2) for sb in slabs], axis=0)

        def kb():
            src = k_ref if n else kp_ref
            return jnp.concatenate([src[0, 0, sb, prv] for sb in slabs] + [k_ref[0, 0, sb, cur] for sb in slabs], axis=0)

        def v2():
            prev_slot = slot if n else 1 - slot
            return jnp.concatenate(
                [vm_ref[hh, prev_slot, sb, prv] if part == 0 else vm_ref[hh, slot, sb, cur]
                 for hh in range(2) for part in range(2) for sb in slabs], axis=0)

        def store(acc, den, m_full):
            for a, sb in enumerate(slabs):
                piece = slice(a * rows4, (a + 1) * rows4)
                acc_ref[0, sb, cur] = acc[piece]
                den_ref[0, sb, cur] = den[piece]
                mx_ref[0, sb, cur] = m_full[piece]
        return 1, (first_tile if n == 0 else 0), q2, kb, v2, store

    def unit_d1(n):
        q_rows = slice(n * SUBLANES, (n + 1) * SUBLANES)
        k_rows = slice(n * SUBLANES, (n + 2) * SUBLANES)
        tok = n * BAND

        def q2():
            return jnp.concatenate(split_heads(qf_ref[:, q_rows, :].reshape(BAND, LANES)), axis=0).astype(BF16)

        def kb():
            return kf_ref[:, k_rows, :].reshape(2 * BAND, LANES).astype(BF16)

        def v2():
            return jnp.concatenate(split_heads(vf_ref[:, k_rows, :].reshape(2 * BAND, LANES)), axis=0).astype(BF16)

        def store(acc, den, m_full):
            others = [tuple(ref[p, :, q_rows, :].reshape(BAND, LANES) for ref in (acc_ref, den_ref, mx_ref))
                      for p in range(2)]
            m_all = jnp.maximum(jnp.maximum(m_full, others[0][2]), others[1][2])
            w = jnp.exp2(m_full - m_all)
            num = w * acc
            total = w * den
            for acc_p, den_p, m_p in others:
                w = jnp.exp2(m_p - m_all)
                num = num + w * acc_p
                total = total + w * den_p
            y = (num / total).reshape(RES, SUBLANES, LANES)
            for r in range(RES):
                nat_ref[pl.ds(tok + r, SUBLANES, stride=RES), :] = y[r]
            rows = slice(tok, tok + BAND)
            gate = _silu(g_ref[0, rows, :].astype(F32))
            o_ref[0, rows, :] = (nat_ref[rows, :] * gate).astype(BF16)
        return 0, (first_tile if n == 0 else 0), q2, kb, v2, store

    units = [unit_d16(r) for r in range(RES)]
    d1_ready = []
    for n4 in range(4):
        for c in range(4):
            units.append(unit_d4(c, n4))
            if d1_ready:
                units.append(unit_d1(d1_ready.pop(0)))
        d1_ready.extend(range(4 * n4, 4 * n4 + 4))
    units.extend(unit_d1(n) for n in d1_ready)

    s, pm = {}, {}
    sm_lag, pv_lag = 1, 3
    for i in range(len(units) + pv_lag):
        if i == 4:
            unpack_for_d1()
        if i < len(units):
            s[i] = _scores(units[i][2](), units[i][3]())
        if 0 <= i - sm_lag < len(units):
            p, var = units[i - sm_lag][:2]
            pm[i - sm_lag] = _probs(s.pop(i - sm_lag), bias_ref[p, 0, var], bias_ref[p, 1, var], head0)
        if 0 <= i - pv_lag < len(units):
            pc, m_full = pm.pop(i - pv_lag)
            od = pv(pc, units[i - pv_lag][4]())
            units[i - pv_lag][5](od[:, :LANES], od[:, LANES:], m_full)


def _attn(za, zh, slopes, *, batch, seq):
    nt_ = seq // ATTN_TILE
    n_pat = len(DILATED_PATTERNS)
    blk = (1, 1, RES, BAND, LANES)

    def zspec(cb):
        return pl.BlockSpec(blk, lambda b, h, t, sl: (cb + h, b * nt_ + t, 0, 0, 0))

    prev_keys = pl.BlockSpec(blk, lambda b, h, t, sl: (CB_AK + h, b * nt_ + jnp.maximum(t - 1, 0), 0, 0, 0))
    gate = pl.BlockSpec((1, ATTN_TILE, LANES), lambda b, h, t, sl: (CB_AG + h, b * nt_ + t, 0))
    grid_spec = pltpu.PrefetchScalarGridSpec(
        num_scalar_prefetch=1,
        grid=(batch, HEAD_PAIRS, nt_),
        in_specs=[zspec(CB_AQ), zspec(CB_AK), zspec(CB_AV), prev_keys, gate],
        out_specs=pl.BlockSpec((1, ATTN_TILE, LANES), lambda b, h, t, sl: (h, b * nt_ + t, 0)),
        scratch_shapes=[
            pltpu.VMEM((n_pat, 2, 2, BAND, 2 * BAND), F32),
            pltpu.VMEM((2, RES, BAND, LANES), BF16),
            pltpu.VMEM((2, 2, RES, BAND, LANES), BF16),
            pltpu.VMEM((RES, BAND, LANES), F32),
            pltpu.VMEM((RES, SUBLANES + BAND, LANES), F32),
            pltpu.VMEM((RES, SUBLANES + BAND, LANES), F32),
            pltpu.VMEM((n_pat - 1, RES, BAND, LANES), F32),
            pltpu.VMEM((n_pat - 1, RES, BAND, LANES), F32),
            pltpu.VMEM((n_pat - 1, RES, BAND, LANES), F32),
            pltpu.VMEM((ATTN_TILE, LANES), F32),
        ],
    )
    return pl.pallas_call(
        _attn_kernel,
        grid_spec=grid_spec,
        out_shape=jax.ShapeDtypeStruct((HEAD_PAIRS, batch * seq, LANES), BF16),
        compiler_params=pltpu.CompilerParams(
            dimension_semantics=("parallel", "parallel", "arbitrary"), vmem_limit_bytes=VMEM_LIMIT),
        name="dilated_attn",
    )(slopes, za, za, za, za, zh)


def _outproj_kernel(yh_ref, ya_ref, x_ref, w_ref, g_ref, o_ref):
    y = jnp.concatenate([yh_ref[c] for c in range(yh_ref.shape[0])]
                        + [ya_ref[c] for c in range(ya_ref.shape[0])], axis=-1)
    r = x_ref[...] + jnp.dot(y, w_ref[...], preferred_element_type=F32)
    ms = jnp.mean(r * r, axis=-1, keepdims=True)
    o_ref[...] = r * lax.rsqrt(ms + NORM_EPS) * g_ref[...]


def _outproj(yh3, ya3, x2, w_bf16, gain, *, tm):
    m, d = x2.shape
    return pl.pallas_call(
        _outproj_kernel,
        grid=(m // tm,),
        in_specs=[
            pl.BlockSpec((yh3.shape[0], tm, LANES), lambda i: (0, i, 0)),
            pl.BlockSpec((ya3.shape[0], tm, LANES), lambda i: (0, i, 0)),
            pl.BlockSpec((tm, d), lambda i: (i, 0)),
            pl.BlockSpec(w_bf16.shape, lambda i: (0, 0)),
            pl.BlockSpec((1, d), lambda i: (0, 0)),
        ],
        out_specs=pl.BlockSpec((tm, d), lambda i: (i, 0)),
        out_shape=jax.ShapeDtypeStruct((m, d), F32),
        compiler_params=pltpu.CompilerParams(
            dimension_semantics=("parallel",), vmem_limit_bytes=VMEM_LIMIT),
        name="outproj",
    )(yh3, ya3, x2, w_bf16, gain)


def kernel(x, norm_gain, w_in, lb_logits, hgrn_gnorm, w_out, final_gain):
    batch, seq, d = x.shape
    assert d == D_MODEL and w_in.shape == (1, D_MODEL, 4 * D_HGRN + 4 * D_ATTN) and lb_logits.shape[0] == 2
    m = batch * seq
    assert seq % ATTN_TILE == 0

    x2 = x.reshape(m, d)
    w_in_bf16 = w_in[0].astype(BF16)
    gain = norm_gain[0:1]
    zh = _inproj(x2, gain, w_in_bf16, tm=1024, tn=D_HGRN, n_col_tiles=5,
                 col_tile=lambda j: jnp.where(j < 4, j, 7), by_residue=False)
    w_qkv = w_in_bf16[:, 4 * D_HGRN:4 * D_HGRN + 3 * D_ATTN]
    za, yh3 = _inproj(x2, gain, w_qkv, tm=1024, tn=3 * D_ATTN // 4, n_col_tiles=4, col_tile=lambda j: j,
                      by_residue=True, hgrn=(zh, lb_logits, hgrn_gnorm[0:1], batch, seq, 1024))
    slopes = jnp.exp2(-8.0 * jnp.arange(1, ATTN_HEADS + 1, dtype=F32) / ATTN_HEADS)
    ya3 = _attn(za, zh, slopes, batch=batch, seq=seq)
    out = _outproj(yh3, ya3, x2, w_out[0].astype(BF16), final_gain.reshape(1, d), tm=512)
    return out.reshape(batch, seq, d)
```

```python
import functools

import jax
import jax.numpy as jnp
from jax import lax
from jax.experimental import pallas as pl
from jax.experimental.pallas import tpu as pltpu

D_MODEL = 2048
D_HGRN = 1024
HGRN_HEAD = 128
HGRN_HEADS = D_HGRN // HGRN_HEAD
HGRN_CHUNK = 64
D_ATTN = 1024
ATTN_HEAD = 64
ATTN_HEADS = D_ATTN // ATTN_HEAD
DILATED_PATTERNS = ((128, 1), (512, 4), (2048, 16))
NORM_EPS = 1e-6

LANES = 128
SUBLANES = 8
BAND = 128
HEAD_PAIRS = ATTN_HEADS // 2
MASK_VALUE = -1e30
LOG2_E = 1.4426950408889634
RES = 16
ATTN_TILE = BAND * RES

CB_HQ, CB_HF, CB_HI, CB_HG, CB_AG = 0, 8, 16, 24, 32
CB_AQ, CB_AK, CB_AV = 0, 8, 16

VMEM_LIMIT = 56 * 1024 * 1024

F32 = jnp.float32
BF16 = jnp.bfloat16
NT_DIMS = (((1,), (1,)), ((), ()))
TN_DIMS = (((0,), (0,)), ((), ()))


def _silu(v):
    u = 0.5 * v
    return u * jnp.tanh(u) + u


def _interleave(pieces, phases, advance_after):
    for _ in range(advance_after.get(-1, 0)):
        next(phases, None)
    for k, piece in enumerate(pieces):
        piece()
        for _ in range(advance_after.get(k, 0)):
            next(phases, None)
    for _ in phases:
        pass


def _inproj_kernel(x_ref, *rest, by_residue, q_cols, q_scale, hgrn_steps_per_seq, first_tile_parts):
    if by_residue:
        (g_ref, w_ref, zq_ref, zf_ref, zi_ref, zg_ref, lbl_ref, gn_ref, o_ref, yo_ref,
         h_ref, xs_ref, y_ref, st_ref) = rest
    else:
        g_ref, w_ref, o_ref, h_ref = rest
    tm, d = h_ref.shape
    n_lane_blocks = d // LANES
    tn = w_ref.shape[1]
    per_res = tm // RES
    chunk = tm // 4
    i = pl.program_id(0)
    j = pl.program_id(1)
    sub = 2 * LANES

    def project(c, h_rows, row0, scale):
        n_rows = h_rows.shape[0]
        acc = jnp.dot(h_rows, w_ref[:, c * sub:(c + 1) * sub], preferred_element_type=F32)
        if scale is not None:
            acc = acc * scale
        for half in range(2):
            part = acc[:, half * LANES:(half + 1) * LANES].astype(BF16)
            if by_residue:
                for k in range(n_rows // per_res):
                    blk = row0 // per_res + k
                    o_ref[2 * c + half, 0, 4 * (blk % 4) + blk // 4] = part[k * per_res:(k + 1) * per_res]
            else:
                o_ref[2 * c + half, row0:row0 + n_rows] = part

    def normalise_quarter(c4):
        rows = slice(c4 * chunk, (c4 + 1) * chunk)
        if by_residue:
            if c4 == 0:
                for c in range(n_lane_blocks):
                    xs_ref[c] = x_ref[:, c * LANES:(c + 1) * LANES]
            for c in range(n_lane_blocks):
                y_ref[c] = xs_ref[c, pl.ds(c4, chunk, stride=4), :]
            for a in range(4):
                src = pl.ds(a, per_res, stride=4)
                dst = slice((4 * c4 + a) * per_res, (4 * c4 + a + 1) * per_res)
                cols = [y_ref[c, src, :] for c in range(n_lane_blocks)]
                ssq = cols[0] * cols[0]
                for xc in cols[1:]:
                    ssq = ssq + xc * xc
                scale = lax.rsqrt(jnp.sum(ssq, axis=-1, keepdims=True) * (1.0 / d) + NORM_EPS)
                for c, xc in enumerate(cols):
                    lanes = slice(c * LANES, (c + 1) * LANES)
                    h_ref[dst, lanes] = (xc * scale * g_ref[:, lanes]).astype(BF16)
        else:
            xv = x_ref[rows, :]
            ms = jnp.mean(xv * xv, axis=-1, keepdims=True)
            h_ref[rows, :] = (xv * lax.rsqrt(ms + NORM_EPS) * g_ref[...]).astype(BF16)

    def hgrn():
        step = i * pl.num_programs(1) + j
        first = lax.rem(step, hgrn_steps_per_seq) == 0
        return _hgrn_phases(zq_ref, zf_ref, zi_ref, zg_ref, lbl_ref, gn_ref, yo_ref, st_ref, first)

    def run(pieces):
        if by_residue:
            n = len(pieces)
            _interleave(pieces, hgrn(), {-1: 1, n // 3 - 1: 2, 2 * n // 3 - 1: 2, n - 1: 2})
        else:
            for piece in pieces:
                piece()

    @pl.when(j == 0)
    def _():
        pieces = []
        quarters_per_part = 4 // first_tile_parts
        for part in range(first_tile_parts):
            rows = slice(part * quarters_per_part * chunk, (part + 1) * quarters_per_part * chunk)
            for c in range(tn // sub):
                scale = q_scale if c * sub < q_cols else None

                def piece(part=part, c=c, rows=rows, scale=scale):
                    if c == 0:
                        for c4 in range(part * quarters_per_part, (part + 1) * quarters_per_part):
                            normalise_quarter(c4)
                    project(c, h_ref[rows, :], rows.start, scale)
                pieces.append(piece)
        run(pieces)

    @pl.when(j > 0)
    def _():
        pieces = []
        for c in range(tn // sub):
            n_q_tiles = -(-(q_cols - c * sub) // tn) if q_cols > c * sub else 0
            scale = jnp.where(j < n_q_tiles, q_scale, 1.0) if n_q_tiles > 1 else None
            pieces.append(lambda c=c, scale=scale: project(c, h_ref[...], 0, scale))
        run(pieces)


def _inproj(x2, gain, w_bf16, *, tm, tn, n_col_tiles, col_tile, by_residue, hgrn=None):
    m, d = x2.shape
    n = n_col_tiles * tn
    grid = (m // tm, n_col_tiles)
    scratch = [pltpu.VMEM((tm, d), BF16)]
    extra_in, extra_specs, out_shapes, out_specs = [], [], [], []
    if by_residue:
        halves = ATTN_TILE // tm
        out_shapes.append(jax.ShapeDtypeStruct((n // LANES, m // ATTN_TILE, RES, BAND, LANES), BF16))
        out_specs.append(pl.BlockSpec((tn // LANES, 1, RES, tm // RES, LANES),
                                      lambda i, j: (j, i // halves, 0, i % halves, 0)))
        scratch.append(pltpu.VMEM((d // LANES, tm, LANES), F32))
        scratch.append(pltpu.VMEM((d // LANES, tm // 4, LANES), F32))

        zh, lb_logits, gnorm, batch, seq, ts = hgrn
        nh = 2
        pairs, seq_tiles = HGRN_HEADS // nh, seq // ts
        assert grid[0] * grid[1] == batch * pairs * seq_tiles

        def where(i, j):
            step = i * n_col_tiles + j
            return (step // seq_tiles) % pairs, (step // (pairs * seq_tiles)) * seq_tiles + step % seq_tiles

        def zspec(cb):
            return pl.BlockSpec((nh, ts, LANES), lambda i, j: (cb // nh + where(i, j)[0], where(i, j)[1], 0))

        extra_in = [zh, zh, zh, zh, lb_logits, gnorm]
        extra_specs = [zspec(CB_HQ), zspec(CB_HF), zspec(CB_HI), zspec(CB_HG),
                       pl.BlockSpec((lb_logits.shape[0], nh * LANES), lambda i, j: (0, where(i, j)[0])),
                       pl.BlockSpec((1, LANES), lambda i, j: (0, 0))]
        out_shapes.append(jax.ShapeDtypeStruct((HGRN_HEADS, batch * seq, LANES), BF16))
        out_specs.append(pl.BlockSpec((nh, ts, LANES), lambda i, j: (where(i, j)[0], where(i, j)[1], 0)))
        scratch.append(pltpu.VMEM((nh, HGRN_HEAD, HGRN_HEAD), F32))
        steps_per_seq = seq_tiles
    else:
        out_shapes.append(jax.ShapeDtypeStruct((n // LANES, m, LANES), BF16))
        out_specs.append(pl.BlockSpec((tn // LANES, tm, LANES), lambda i, j: (j, i, 0)))
        steps_per_seq = None
    outs = pl.pallas_call(
        functools.partial(_inproj_kernel, by_residue=by_residue,
                          q_cols=D_ATTN if by_residue else 0, q_scale=ATTN_HEAD ** -0.5 * LOG2_E,
                          hgrn_steps_per_seq=steps_per_seq, first_tile_parts=4),
        grid=grid,
        in_specs=[
            pl.BlockSpec((tm, d), lambda i, j: (i, 0)),
            pl.BlockSpec((1, d), lambda i, j: (0, 0)),
            pl.BlockSpec((d, tn), lambda i, j: (0, col_tile(j))),
        ] + extra_specs,
        out_specs=out_specs,
        out_shape=out_shapes,
        scratch_shapes=scratch,
        compiler_params=pltpu.CompilerParams(
            dimension_semantics=("arbitrary", "arbitrary") if by_residue else ("parallel", "arbitrary"),
            vmem_limit_bytes=VMEM_LIMIT),
        name="inproj_attn_hgrn2" if by_residue else "inproj_token",
    )(x2, gain, w_bf16, *extra_in)
    return outs if by_residue else outs[0]


def _hgrn_phases(q_ref, f_ref, i_ref, g_ref, lbl_ref, gn_ref, o_ref, st_ref, first_of_sequence):
    nh, ts, _ = q_ref.shape
    heads = range(nh)
    c_len = HGRN_CHUNK
    chunks = [slice(c * c_len, (c + 1) * c_len) for c in range(ts // c_len)]
    row = lax.broadcasted_iota(jnp.int32, (c_len, c_len), 0)
    col = lax.broadcasted_iota(jnp.int32, (c_len, c_len), 1)
    causal = col <= row
    tril = causal.astype(BF16)

    lbl = lbl_ref[...]
    e = jnp.exp(lbl - jnp.max(lbl, axis=0, keepdims=True))
    lb_all = e[0:1, :] / jnp.sum(e, axis=0, keepdims=True)
    lb = [lb_all[:, h * LANES:(h + 1) * LANES] for h in heads]

    q = [_silu(q_ref[h].astype(F32)) for h in heads]
    half_range = [0.5 * (1.0 - lb[h]) for h in heads]
    mid = [lb[h] + half_range[h] for h in heads]
    f = [mid[h] + half_range[h] * jnp.tanh(0.5 * f_ref[h].astype(F32)) for h in heads]
    k = [1.0 - f[h] for h in heads]
    log_f = [jnp.log2(f[h]) for h in heads]
    hilo = []
    for h in heads:
        hi = log_f[h].astype(BF16)
        lo = (log_f[h] - hi.astype(F32)).astype(BF16)
        hilo.append(jnp.concatenate([hi, lo], axis=-1))
    yield
    cs = [jnp.concatenate([jnp.dot(tril, hilo[h][sl], preferred_element_type=F32) for sl in chunks], axis=0)
          for h in heads]
    yield
    b_cum =[cs[h][:, :LANES] + cs[h][:, LANES:] for h in heads]
    decay = [[jnp.exp2(b_cum[h][sl.stop - 1:sl.stop, :]) for sl in chunks] for h in heads]
    q_dec = [(q[h] * jnp.exp2(b_cum[h])).astype(BF16) for h in heads]
    k_dec_f = [k[h] * jnp.exp2(-b_cum[h]) for h in heads]
    k_dec = [k_dec_f[h].astype(BF16) for h in heads]
    k_end = [jnp.concatenate([k_dec_f[h][sl] * decay[h][c] for c, sl in enumerate(chunks)], axis=0).astype(BF16)
             for h in heads]
    v = [i_ref[h] for h in heads]
    yield
    kv_t = [[lax.dot_general(v[h][sl], k_end[h][sl], TN_DIMS, preferred_element_type=F32) for sl in chunks]
            for h in heads]
    a = [[lax.dot_general(q_dec[h][sl], k_dec[h][sl], NT_DIMS, preferred_element_type=F32) for sl in chunks]
         for h in heads]
    yield
    a = [[jnp.where(causal, a_c, 0.0).astype(BF16) for a_c in a[h]] for h in heads]
    entering = [[] for _ in heads]
    for h in heads:
        st = jnp.where(first_of_sequence, 0.0, st_ref[h])
        for c in range(len(chunks)):
            entering[h].append(st.astype(BF16))
            st = st * decay[h][c] + kv_t[h][c]
        st_ref[h] = st
    yield
    o = [jnp.concatenate(
        [jnp.dot(a[h][c], v[h][sl], preferred_element_type=F32)
         + lax.dot_general(q_dec[h][sl], entering[h][c], NT_DIMS, preferred_element_type=F32)
         for c, sl in enumerate(chunks)], axis=0) for h in heads]
    yield
    gn = gn_ref[...]
    for h in heads:
        o_h = o[h] * lax.rsqrt(jnp.mean(o[h] * o[h], axis=-1, keepdims=True) + NORM_EPS) * gn
        o_ref[h] = (o_h * _silu(g_ref[h].astype(F32))).astype(BF16)


def _unit_positions(p):
    rho = lax.broadcasted_iota(jnp.int32, (BAND, 2 * BAND), 0)
    kap = lax.broadcasted_iota(jnp.int32, (BAND, 2 * BAND), 1)
    if p == 2:
        return BAND + rho, kap
    if p == 1:
        q_pos = BAND + 4 * (rho & 31) + (rho >> 5)
        k_pos = (kap >> 7) * BAND + 4 * (kap & 31) + ((kap >> 5) & 3)
        return q_pos, k_pos
    q_pos = BAND + RES * (rho & 7) + (rho >> 3)
    k_pos = RES * (kap & 15) + (kap >> 4)
    return q_pos, k_pos


def _scores(q2, kb):
    return lax.dot_general(q2, kb, NT_DIMS, preferred_element_type=F32)


def _probs(s, bias0, bias1, head0):
    s0 = s[:BAND] + bias0
    m0 = jnp.max(s0, axis=-1, keepdims=True)
    p0 = jnp.exp2(s0 - m0).astype(BF16)
    s1 = s[BAND:] + bias1
    m1 = jnp.max(s1, axis=-1, keepdims=True)
    p1 = jnp.exp2(s1 - m1).astype(BF16)
    return jnp.concatenate([p0, p1], axis=1), jnp.where(head0, m0, m1)


def _attn_kernel(slopes_ref, q_ref, k_ref, v_ref, kp_ref, g_ref, o_ref,
                 bias_ref, qm_ref, vm_ref, qf_ref, kf_ref, vf_ref, acc_ref, mx_ref, den_ref, nat_ref):
    hp = pl.program_id(1)
    tile = pl.program_id(2)
    first_tile = jnp.where(tile == 0, 1, 0)
    slot = lax.bitwise_and(tile, 1)
    lane = lax.broadcasted_iota(jnp.int32, (1, LANES), 1)
    head0 = lane < ATTN_HEAD
    prev_rows = SUBLANES

    @pl.when(tile == 0)
    def _():
        vm_ref[0, 1] = jnp.zeros((RES, BAND, LANES), BF16)
        vm_ref[1, 1] = jnp.zeros((RES, BAND, LANES), BF16)
        for p, (_, dil) in enumerate(DILATED_PATTERNS):
            q_pos, k_pos = _unit_positions(p)
            delta = q_pos - k_pos
            in_band = (delta >= 0) & (delta <= BAND)
            not_prev = k_pos >= BAND
            delta_f = delta.astype(F32)
            for hh in range(2):
                slope = slopes_ref[2 * hp + hh]
                bias = -(slope * (float(dil) * LOG2_E)) * delta_f
                bias_ref[p, hh, 0] = jnp.where(in_band, bias, MASK_VALUE)
                bias_ref[p, hh, 1] = jnp.where(in_band & not_prev, bias, MASK_VALUE)

    def split_heads(x):
        zero = jnp.zeros_like(x)
        return jnp.where(head0, x, zero), jnp.where(head0, zero, x)

    qm_ref[0], qm_ref[1] = split_heads(q_ref[0, 0])
    vm_ref[0, slot], vm_ref[1, slot] = split_heads(v_ref[0, 0])

    def unpack_for_d1():
        qf_ref[...] = q_ref[0, 0].astype(F32)
        packed_rows = 2 * SUBLANES
        v_tail = (vm_ref[0, 1 - slot, :, BAND - packed_rows:, :].astype(F32)
                  + vm_ref[1, 1 - slot, :, BAND - packed_rows:, :].astype(F32))
        vf_ref[:, :prev_rows, :] = v_tail[:, packed_rows - prev_rows:, :]
        vf_ref[:, prev_rows:, :] = v_ref[0, 0].astype(F32)
        k_tail = kp_ref[0, 0, :, BAND - packed_rows:, :].astype(F32)
        kf_ref[:, :prev_rows, :] = k_tail[:, packed_rows - prev_rows:, :]
        kf_ref[:, prev_rows:, :] = k_ref[0, 0].astype(F32)

    key_row = lax.broadcasted_iota(jnp.int32, (4 * BAND, LANES), 0)
    key_lane = lax.broadcasted_iota(jnp.int32, (4 * BAND, LANES), 1)
    den_cols = ((key_row < 2 * BAND) == (key_lane < ATTN_HEAD)).astype(BF16)

    def pv(pc, v2):
        return jnp.dot(pc, jnp.concatenate([v2, den_cols], axis=1), preferred_element_type=F32)

    rows4 = BAND // 4

    def unit_d16(r):
        def q2():
            return jnp.concatenate([qm_ref[0, r], qm_ref[1, r]], axis=0)

        def kb():
            return jnp.concatenate([kp_ref[0, 0, r], k_ref[0, 0, r]], axis=0)

        def v2():
            return jnp.concatenate([vm_ref[0, 1 - slot, r], vm_ref[0, slot, r],
                                    vm_ref[1, 1 - slot, r], vm_ref[1, slot, r]], axis=0)

        def store(acc, den, m_full):
            acc_ref[1, r] = acc
            den_ref[1, r] = den
            mx_ref[1, r] = m_full
        return 2, first_tile, q2, kb, v2, store

    def unit_d4(c, n):
        cur = slice(n * rows4, (n + 1) * rows4)
        prv = slice((n - 1) * rows4, n * rows4) if n else slice(BAND - rows4, BAND)
        slabs = [4 * a + c for a in range(4)]

        def q2():
            return jnp.concatenate([qm_ref[hh, sb, cur] for hh in range(2) for sb in slabs], axis=0)

        def kb():
            src = k_ref if n else kp_ref
            return jnp.concatenate([src[0, 0, sb, prv] for sb in slabs] + [k_ref[0, 0, sb, cur] for sb in slabs], axis=0)

        def v2():
            prev_slot = slot if n else 1 - slot
            return jnp.concatenate(
                [vm_ref[hh, prev_slot, sb, prv] if part == 0 else vm_ref[hh, slot, sb, cur]
                 for hh in range(2) for part in range(2) for sb in slabs], axis=0)

        def store(acc, den, m_full):
            for a, sb in enumerate(slabs):
                piece = slice(a * rows4, (a + 1) * rows4)
                acc_ref[0, sb, cur] = acc[piece]
                den_ref[0, sb, cur] = den[piece]
                mx_ref[0, sb, cur] = m_full[piece]
        return 1, (first_tile if n == 0 else 0), q2, kb, v2, store

    def unit_d1(n):
        q_rows = slice(n * SUBLANES, (n + 1) * SUBLANES)
        k_rows = slice(n * SUBLANES, (n + 2) * SUBLANES)
        tok = n * BAND

        def q2():
            return jnp.concatenate(split_heads(qf_ref[:, q_rows, :].reshape(BAND, LANES)), axis=0).astype(BF16)

        def kb():
            return kf_ref[:, k_rows, :].reshape(2 * BAND, LANES).astype(BF16)

        def v2():
            return jnp.concatenate(split_heads(vf_ref[:, k_rows, :].reshape(2 * BAND, LANES)), axis=0).astype(BF16)

        def store(acc, den, m_full):
            others = [tuple(ref[p, :, q_rows, :].reshape(BAND, LANES) for ref in (acc_ref, den_ref, mx_ref))
                      for p in range(2)]
            m_all = jnp.maximum(jnp.maximum(m_full, others[0][2]), others[1][2])
            w = jnp.exp2(m_full - m_all)
            num = w * acc
            total = w * den
            for acc_p, den_p, m_p in others:
                w = jnp.exp2(m_p - m_all)
                num = num + w * acc_p
                total = total + w * den_p
            y = (num / total).reshape(RES, SUBLANES, LANES)
            for r in range(RES):
                nat_ref[pl.ds(tok + r, SUBLANES, stride=RES), :] = y[r]
            rows = slice(tok, tok + BAND)
            gate = _silu(g_ref[0, rows, :].astype(F32))
            o_ref[0, rows, :] = (nat_ref[rows, :] * gate).astype(BF16)
        return 0, (first_tile if n == 0 else 0), q2, kb, v2, store

    units = [unit_d16(r) for r in range(RES)]
    d1_ready = []
    for n4 in range(4):
        for c in range(4):
            units.append(unit_d4(c, n4))
            if d1_ready:
                units.append(unit_d1(d1_ready.pop(0)))
        d1_ready.extend(range(4 * n4, 4 * n4 + 4))
    units.extend(unit_d1(n) for n in d1_ready)

    s, pm = {}, {}
    sm_lag, pv_lag = 1, 3
    for i in range(len(units) + pv_lag):
        if i == 4:
            unpack_for_d1()
        if i < len(units):
            s[i] = _scores(units[i][2](), units[i][3]())
        if 0 <= i - sm_lag < len(units):
            p, var = units[i - sm_lag][:2]
            pm[i - sm_lag] = _probs(s.pop(i - sm_lag), bias_ref[p, 0, var], bias_ref[p, 1, var], head0)
        if 0 <= i - pv_lag < len(units):
            pc, m_full = pm.pop(i - pv_lag)
            od = pv(pc, units[i - pv_lag][4]())
            units[i - pv_lag][5](od[:, :LANES], od[:, LANES:], m_full)


def _attn(za, zh, slopes, *, batch, seq):
    nt_ = seq // ATTN_TILE
    n_pat = len(DILATED_PATTERNS)
    blk = (1, 1, RES, BAND, LANES)

    def zspec(cb):
        return pl.BlockSpec(blk, lambda b, h, t, sl: (cb + h, b * nt_ + t, 0, 0, 0))

    prev_keys = pl.BlockSpec(blk, lambda b, h, t, sl: (CB_AK + h, b * nt_ + jnp.maximum(t - 1, 0), 0, 0, 0))
    gate = pl.BlockSpec((1, ATTN_TILE, LANES), lambda b, h, t, sl: (CB_AG + h, b * nt_ + t, 0))
    grid_spec = pltpu.PrefetchScalarGridSpec(
        num_scalar_prefetch=1,
        grid=(batch, HEAD_PAIRS, nt_),
        in_specs=[zspec(CB_AQ), zspec(CB_AK), zspec(CB_AV), prev_keys, gate],
        out_specs=pl.BlockSpec((1, ATTN_TILE, LANES), lambda b, h, t, sl: (h, b * nt_ + t, 0)),
        scratch_shapes=[
            pltpu.VMEM((n_pat, 2, 2, BAND, 2 * BAND), F32),
            pltpu.VMEM((2, RES, BAND, LANES), BF16),
            pltpu.VMEM((2, 2, RES, BAND, LANES), BF16),
            pltpu.VMEM((RES, BAND, LANES), F32),
            pltpu.VMEM((RES, SUBLANES + BAND, LANES), F32),
            pltpu.VMEM((RES, SUBLANES + BAND, LANES), F32),
            pltpu.VMEM((n_pat - 1, RES, BAND, LANES), F32),
            pltpu.VMEM((n_pat - 1, RES, BAND, LANES), F32),
            pltpu.VMEM((n_pat - 1, RES, BAND, LANES), F32),
            pltpu.VMEM((ATTN_TILE, LANES), F32),
        ],
    )
    return pl.pallas_call(
        _attn_kernel,
        grid_spec=grid_spec,
        out_shape=jax.ShapeDtypeStruct((HEAD_PAIRS, batch * seq, LANES), BF16),
        compiler_params=pltpu.CompilerParams(
            dimension_semantics=("parallel", "parallel", "arbitrary"), vmem_limit_bytes=VMEM_LIMIT),
        name="dilated_attn",
    )(slopes, za, za, za, za, zh)


def _outproj_kernel(yh_ref, ya_ref, x_ref, w_ref, g_ref, o_ref):
    y = jnp.concatenate([yh_ref[c] for c in range(yh_ref.shape[0])]
                        + [ya_ref[c] for c in range(ya_ref.shape[0])], axis=-1)
    r = x_ref[...] + jnp.dot(y, w_ref[...], preferred_element_type=F32)
    ms = jnp.mean(r * r, axis=-1, keepdims=True)
    o_ref[...] = r * lax.rsqrt(ms + NORM_EPS) * g_ref[...]


def _outproj(yh3, ya3, x2, w_bf16, gain, *, tm):
    m, d = x2.shape
    return pl.pallas_call(
        _outproj_kernel,
        grid=(m // tm,),
        in_specs=[
            pl.BlockSpec((yh3.shape[0], tm, LANES), lambda i: (0, i, 0)),
            pl.BlockSpec((ya3.shape[0], tm, LANES), lambda i: (0, i, 0)),
            pl.BlockSpec((tm, d), lambda i: (i, 0)),
            pl.BlockSpec(w_bf16.shape, lambda i: (0, 0)),
            pl.BlockSpec((1, d), lambda i: (0, 0)),
        ],
        out_specs=pl.BlockSpec((tm, d), lambda i: (i, 0)),
        out_shape=jax.ShapeDtypeStruct((m, d), F32),
        compiler_params=pltpu.CompilerParams(
            dimension_semantics=("parallel",), vmem_limit_bytes=VMEM_LIMIT),
        name="outproj",
    )(yh3, ya3, x2, w_bf16, gain)


def kernel(x, norm_gain, w_in, lb_logits, hgrn_gnorm, w_out, final_gain):
    batch, seq, d = x.shape
    assert d == D_MODEL and w_in.shape == (1, D_MODEL, 4 * D_HGRN + 4 * D_ATTN) and lb_logits.shape[0] == 2
    m = batch * seq
    assert seq % ATTN_TILE == 0

    x2 = x.reshape(m, d)
    w_in_bf16 = w_in[0].astype(BF16)
    gain = norm_gain[0:1]
    zh = _inproj(x2, gain, w_in_bf16, tm=1024, tn=D_HGRN, n_col_tiles=5,
                 col_tile=lambda j: jnp.where(j < 4, j, 7), by_residue=False)
    w_qkv = w_in_bf16[:, 4 * D_HGRN:4 * D_HGRN + 3 * D_ATTN]
    za, yh3 = _inproj(x2, gain, w_qkv, tm=1024, tn=3 * D_ATTN // 4, n_col_tiles=4, col_tile=lambda j: j,
                      by_residue=True, hgrn=(zh, lb_logits, hgrn_gnorm[0:1], batch, seq, 1024))
    slopes = jnp.exp2(-8.0 * jnp.arange(1, ATTN_HEADS + 1, dtype=F32) / ATTN_HEADS)
    ya3 = _attn(za, zh, slopes, batch=batch, seq=seq)
    out = _outproj(yh3, ya3, x2, w_out[0].astype(BF16), final_gain.reshape(1, d), tm=512)
    return out.reshape(batch, seq, d)
```

```python
import functools

import jax
import jax.numpy as jnp
from jax import lax
from jax.experimental import pallas as pl
from jax.experimental.pallas import tpu as pltpu

D_MODEL = 2048
D_HGRN = 1024
HGRN_HEAD = 128
HGRN_HEADS = D_HGRN // HGRN_HEAD
HGRN_CHUNK = 64
D_ATTN = 1024
ATTN_HEAD = 64
ATTN_HEADS = D_ATTN // ATTN_HEAD
DILATED_PATTERNS = ((128, 1), (512, 4), (2048, 16))
NORM_EPS = 1e-6

LANES = 128
SUBLANES = 8
BAND = 128
HEAD_PAIRS = ATTN_HEADS // 2
MASK_VALUE = -1e30
LOG2_E = 1.4426950408889634
RES = 16
ATTN_TILE = BAND * RES

CB_HQ, CB_HF, CB_HI, CB_HG, CB_AG = 0, 8, 16, 24, 32
CB_AQ, CB_AK, CB_AV = 0, 8, 16

VMEM_LIMIT = 56 * 1024 * 1024

PROJ_ROWS = 1024
TOKEN_PROJ_COLS = 1024
RESIDUE_PROJ_COLS = 3 * D_ATTN // 4
HGRN_TILE = PROJ_ROWS
HGRN_HEADS_PER_STEP = 2
OUT_ROWS = 512

F32 = jnp.float32
BF16 = jnp.bfloat16
NT_DIMS = (((1,), (1,)), ((), ()))
TN_DIMS = (((0,), (0,)), ((), ()))


def _silu(v):
    u = 0.5 * v
    return u * jnp.tanh(u) + u


def _interleave(pieces, phases, advance_after):
    for _ in range(advance_after.get(-1, 0)):
        next(phases, None)
    for k, piece in enumerate(pieces):
        piece()
        for _ in range(advance_after.get(k, 0)):
            next(phases, None)
    for _ in phases:
        pass


def _inproj_kernel(x_ref, *rest, by_residue, q_cols, q_scale, hgrn_steps_per_seq):
    if by_residue:
        (g_ref, w_ref, zq_ref, zf_ref, zi_ref, zg_ref, lbl_ref, gn_ref, o_ref, yo_ref,
         h_ref, xs_ref, y_ref, st_ref) = rest
    else:
        g_ref, w_ref, o_ref, h_ref = rest
    tm, d = h_ref.shape
    n_lane_blocks = d // LANES
    tn = w_ref.shape[1]
    per_res = tm // RES
    chunk = tm // 4
    i = pl.program_id(0)
    j = pl.program_id(1)
    sub = 2 * LANES

    def project(c, h_rows, row0, scale):
        n_rows = h_rows.shape[0]
        acc = jnp.dot(h_rows, w_ref[:, c * sub:(c + 1) * sub], preferred_element_type=F32)
        if scale is not None:
            acc = acc * scale
        for half in range(2):
            part = acc[:, half * LANES:(half + 1) * LANES].astype(BF16)
            if by_residue:
                for k in range(n_rows // per_res):
                    blk = row0 // per_res + k
                    o_ref[2 * c + half, 0, 4 * (blk % 4) + blk // 4] = part[k * per_res:(k + 1) * per_res]
            else:
                o_ref[2 * c + half, row0:row0 + n_rows] = part

    def normalise_quarter(c4):
        rows = slice(c4 * chunk, (c4 + 1) * chunk)
        if by_residue:
            if c4 == 0:
                for c in range(n_lane_blocks):
                    xs_ref[c] = x_ref[:, c * LANES:(c + 1) * LANES]
            for c in range(n_lane_blocks):
                y_ref[c] = xs_ref[c, pl.ds(c4, chunk, stride=4), :]
            for a in range(4):
                src = pl.ds(a, per_res, stride=4)
                dst = slice((4 * c4 + a) * per_res, (4 * c4 + a + 1) * per_res)
                cols = [y_ref[c, src, :] for c in range(n_lane_blocks)]
                ssq = cols[0] * cols[0]
                for xc in cols[1:]:
                    ssq = ssq + xc * xc
                scale = lax.rsqrt(jnp.sum(ssq, axis=-1, keepdims=True) * (1.0 / d) + NORM_EPS)
                for c, xc in enumerate(cols):
                    lanes = slice(c * LANES, (c + 1) * LANES)
                    h_ref[dst, lanes] = (xc * scale * g_ref[:, lanes]).astype(BF16)
        else:
            xv = x_ref[rows, :]
            ms = jnp.mean(xv * xv, axis=-1, keepdims=True)
            h_ref[rows, :] = (xv * lax.rsqrt(ms + NORM_EPS) * g_ref[...]).astype(BF16)

    def hgrn():
        step = i * pl.num_programs(1) + j
        first = lax.rem(step, hgrn_steps_per_seq) == 0
        return _hgrn_phases(zq_ref, zf_ref, zi_ref, zg_ref, lbl_ref, gn_ref, yo_ref, st_ref, first)

    def run(pieces, slots):
        if by_residue:
            _interleave(pieces, hgrn(), {-1: 1, **{slot: 2 for slot in slots}})
        else:
            for piece in pieces:
                piece()

    @pl.when(j == 0)
    def _():
        pieces = []
        for c4 in range(4):
            rows = slice(c4 * chunk, (c4 + 1) * chunk)
            for c in range(tn // sub):
                scale = q_scale if c * sub < q_cols else None

                def piece(c4=c4, c=c, rows=rows, scale=scale):
                    if c == 0:
                        normalise_quarter(c4)
                    project(c, h_ref[rows, :], rows.start, scale)
                pieces.append(piece)
        per_quarter = tn // sub
        run(pieces, [k * per_quarter - 1 for k in (1, 2, 3)])

    @pl.when(j > 0)
    def _():
        pieces = []
        for c in range(tn // sub):
            n_q_tiles = -(-(q_cols - c * sub) // tn) if q_cols > c * sub else 0
            scale = jnp.where(j < n_q_tiles, q_scale, 1.0) if n_q_tiles > 1 else None
            pieces.append(lambda c=c, scale=scale: project(c, h_ref[...], 0, scale))
        run(pieces, range(3))


def _inproj(x2, gain, w_bf16, *, tm, tn, n_col_tiles, col_tile, by_residue, hgrn=None):
    m, d = x2.shape
    n = n_col_tiles * tn
    grid = (m // tm, n_col_tiles)
    scratch = [pltpu.VMEM((tm, d), BF16)]
    extra_in, extra_specs, out_shapes, out_specs = [], [], [], []
    if by_residue:
        halves = ATTN_TILE // tm
        out_shapes.append(jax.ShapeDtypeStruct((n // LANES, m // ATTN_TILE, RES, BAND, LANES), BF16))
        out_specs.append(pl.BlockSpec((tn // LANES, 1, RES, tm // RES, LANES),
                                      lambda i, j: (j, i // halves, 0, i % halves, 0)))
        scratch.append(pltpu.VMEM((d // LANES, tm, LANES), F32))
        scratch.append(pltpu.VMEM((d // LANES, tm // 4, LANES), F32))

        zh, lb_logits, gnorm, batch, seq, ts = hgrn
        nh = HGRN_HEADS_PER_STEP
        pairs, seq_tiles = HGRN_HEADS // nh, seq // ts
        assert grid[0] * grid[1] == batch * pairs * seq_tiles

        def where(i, j):
            step = i * n_col_tiles + j
            return (step // seq_tiles) % pairs, (step // (pairs * seq_tiles)) * seq_tiles + step % seq_tiles

        def zspec(cb):
            return pl.BlockSpec((nh, ts, LANES), lambda i, j: (cb // nh + where(i, j)[0], where(i, j)[1], 0))

        extra_in = [zh, zh, zh, zh, lb_logits, gnorm]
        extra_specs = [zspec(CB_HQ), zspec(CB_HF), zspec(CB_HI), zspec(CB_HG),
                       pl.BlockSpec((lb_logits.shape[0], nh * LANES), lambda i, j: (0, where(i, j)[0])),
                       pl.BlockSpec((1, LANES), lambda i, j: (0, 0))]
        out_shapes.append(jax.ShapeDtypeStruct((HGRN_HEADS, batch * seq, LANES), BF16))
        out_specs.append(pl.BlockSpec((nh, ts, LANES), lambda i, j: (where(i, j)[0], where(i, j)[1], 0)))
        scratch.append(pltpu.VMEM((nh, HGRN_HEAD, HGRN_HEAD), F32))
        steps_per_seq = seq_tiles
    else:
        out_shapes.append(jax.ShapeDtypeStruct((n // LANES, m, LANES), BF16))
        out_specs.append(pl.BlockSpec((tn // LANES, tm, LANES), lambda i, j: (j, i, 0)))
        steps_per_seq = None
    outs = pl.pallas_call(
        functools.partial(_inproj_kernel, by_residue=by_residue,
                          q_cols=D_ATTN if by_residue else 0, q_scale=ATTN_HEAD ** -0.5 * LOG2_E,
                          hgrn_steps_per_seq=steps_per_seq),
        grid=grid,
        in_specs=[
            pl.BlockSpec((tm, d), lambda i, j: (i, 0)),
            pl.BlockSpec((1, d), lambda i, j: (0, 0)),
            pl.BlockSpec((d, tn), lambda i, j: (0, col_tile(j))),
        ] + extra_specs,
        out_specs=out_specs,
        out_shape=out_shapes,
        scratch_shapes=scratch,
        compiler_params=pltpu.CompilerParams(
            dimension_semantics=("arbitrary", "arbitrary") if by_residue else ("parallel", "arbitrary"),
            vmem_limit_bytes=VMEM_LIMIT),
        name="inproj_attn_hgrn2" if by_residue else "inproj_token",
    )(x2, gain, w_bf16, *extra_in)
    return outs if by_residue else outs[0]


def _hgrn_phases(q_ref, f_ref, i_ref, g_ref, lbl_ref, gn_ref, o_ref, st_ref, first_of_sequence):
    nh, ts, _ = q_ref.shape
    heads = range(nh)
    c_len = HGRN_CHUNK
    chunks = [slice(c * c_len, (c + 1) * c_len) for c in range(ts // c_len)]
    row = lax.broadcasted_iota(jnp.int32, (c_len, c_len), 0)
    col = lax.broadcasted_iota(jnp.int32, (c_len, c_len), 1)
    causal = col <= row
    tril = causal.astype(BF16)

    lbl = lbl_ref[...]
    e = jnp.exp(lbl - jnp.max(lbl, axis=0, keepdims=True))
    lb_all = e[0:1, :] / jnp.sum(e, axis=0, keepdims=True)
    lb = [lb_all[:, h * LANES:(h + 1) * LANES] for h in heads]

    q = [_silu(q_ref[h].astype(F32)) for h in heads]
    half_range = [0.5 * (1.0 - lb[h]) for h in heads]
    mid = [lb[h] + half_range[h] for h in heads]
    f = [mid[h] + half_range[h] * jnp.tanh(0.5 * f_ref[h].astype(F32)) for h in heads]
    k = [1.0 - f[h] for h in heads]
    log_f = [jnp.log2(f[h]) for h in heads]
    hilo = []
    for h in heads:
        hi = log_f[h].astype(BF16)
        lo = (log_f[h] - hi.astype(F32)).astype(BF16)
        hilo.append(jnp.concatenate([hi, lo], axis=-1))
    yield
    cs = [jnp.concatenate([jnp.dot(tril, hilo[h][sl], preferred_element_type=F32) for sl in chunks], axis=0)
          for h in heads]
    yield
    b_cum =[cs[h][:, :LANES] + cs[h][:, LANES:] for h in heads]
    decay = [[jnp.exp2(b_cum[h][sl.stop - 1:sl.stop, :]) for sl in chunks] for h in heads]
    q_dec = [(q[h] * jnp.exp2(b_cum[h])).astype(BF16) for h in heads]
    k_dec_f = [k[h] * jnp.exp2(-b_cum[h]) for h in heads]
    k_dec = [k_dec_f[h].astype(BF16) for h in heads]
    k_end = [jnp.concatenate([k_dec_f[h][sl] * decay[h][c] for c, sl in enumerate(chunks)], axis=0).astype(BF16)
             for h in heads]
    v = [i_ref[h] for h in heads]
    yield
    kv_t = [[lax.dot_general(v[h][sl], k_end[h][sl], TN_DIMS, preferred_element_type=F32) for sl in chunks]
            for h in heads]
    a = [[lax.dot_general(q_dec[h][sl], k_dec[h][sl], NT_DIMS, preferred_element_type=F32) for sl in chunks]
         for h in heads]
    yield
    a = [[jnp.where(causal, a_c, 0.0).astype(BF16) for a_c in a[h]] for h in heads]
    entering = [[] for _ in heads]
    for h in heads:
        st = jnp.where(first_of_sequence, 0.0, st_ref[h])
        for c in range(len(chunks)):
            entering[h].append(st.astype(BF16))
            st = st * decay[h][c] + kv_t[h][c]
        st_ref[h] = st
    yield
    o = [jnp.concatenate(
        [jnp.dot(a[h][c], v[h][sl], preferred_element_type=F32)
         + lax.dot_general(q_dec[h][sl], entering[h][c], NT_DIMS, preferred_element_type=F32)
         for c, sl in enumerate(chunks)], axis=0) for h in heads]
    yield
    gn = gn_ref[...]
    for h in heads:
        o_h = o[h] * lax.rsqrt(jnp.mean(o[h] * o[h], axis=-1, keepdims=True) + NORM_EPS) * gn
        o_ref[h] = (o_h * _silu(g_ref[h].astype(F32))).astype(BF16)


def _unit_positions(p):
    rho = lax.broadcasted_iota(jnp.int32, (BAND, 2 * BAND), 0)
    kap = lax.broadcasted_iota(jnp.int32, (BAND, 2 * BAND), 1)
    if p == 2:
        return BAND + rho, kap
    if p == 1:
        q_pos = BAND + 4 * (rho & 31) + (rho >> 5)
        k_pos = (kap >> 7) * BAND + 4 * (kap & 31) + ((kap >> 5) & 3)
        return q_pos, k_pos
    q_pos = BAND + RES * (rho & 7) + (rho >> 3)
    k_pos = RES * (kap & 15) + (kap >> 4)
    return q_pos, k_pos


def _scores(q2, kb):
    return lax.dot_general(q2, kb, NT_DIMS, preferred_element_type=F32)


def _probs(s, bias0, bias1, head0):
    s0 = s[:BAND] + bias0
    m0 = jnp.max(s0, axis=-1, keepdims=True)
    p0 = jnp.exp2(s0 - m0).astype(BF16)
    s1 = s[BAND:] + bias1
    m1 = jnp.max(s1, axis=-1, keepdims=True)
    p1 = jnp.exp2(s1 - m1).astype(BF16)
    return jnp.concatenate([p0, p1], axis=1), jnp.where(head0, m0, m1)


def _attn_kernel(slopes_ref, q_ref, k_ref, v_ref, kp_ref, g_ref, o_ref,
                 bias_ref, qm_ref, vm_ref, qf_ref, kf_ref, vf_ref, acc_ref, mx_ref, den_ref, nat_ref):
    hp = pl.program_id(1)
    tile = pl.program_id(2)
    first_tile = jnp.where(tile == 0, 1, 0)
    slot = lax.bitwise_and(tile, 1)
    lane = lax.broadcasted_iota(jnp.int32, (1, LANES), 1)
    head0 = lane < ATTN_HEAD
    prev_rows = SUBLANES

    @pl.when(tile == 0)
    def _():
        vm_ref[0, 1] = jnp.zeros((RES, BAND, LANES), BF16)
        vm_ref[1, 1] = jnp.zeros((RES, BAND, LANES), BF16)
        for p, (_, dil) in enumerate(DILATED_PATTERNS):
            q_pos, k_pos = _unit_positions(p)
            delta = q_pos - k_pos
            in_band = (delta >= 0) & (delta <= BAND)
            not_prev = k_pos >= BAND
            delta_f = delta.astype(F32)
            for hh in range(2):
                slope = slopes_ref[2 * hp + hh]
                bias = -(slope * (float(dil) * LOG2_E)) * delta_f
                bias_ref[p, hh, 0] = jnp.where(in_band, bias, MASK_VALUE)
                bias_ref[p, hh, 1] = jnp.where(in_band & not_prev, bias, MASK_VALUE)

    def split_heads(x):
        zero = jnp.zeros_like(x)
        return jnp.where(head0, x, zero), jnp.where(head0, zero, x)

    qm_ref[0], qm_ref[1] = split_heads(q_ref[0, 0])
    vm_ref[0, slot], vm_ref[1, slot] = split_heads(v_ref[0, 0])

    def unpack_for_d1():
        qf_ref[...] = q_ref[0, 0].astype(F32)
        packed_rows = 2 * SUBLANES
        v_tail = (vm_ref[0, 1 - slot, :, BAND - packed_rows:, :].astype(F32)
                  + vm_ref[1, 1 - slot, :, BAND - packed_rows:, :].astype(F32))
        vf_ref[:, :prev_rows, :] = v_tail[:, packed_rows - prev_rows:, :]
        vf_ref[:, prev_rows:, :] = v_ref[0, 0].astype(F32)
        k_tail = kp_ref[0, 0, :, BAND - packed_rows:, :].astype(F32)
        kf_ref[:, :prev_rows, :] = k_tail[:, packed_rows - prev_rows:, :]
        kf_ref[:, prev_rows:, :] = k_ref[0, 0].astype(F32)

    key_row = lax.broadcasted_iota(jnp.int32, (4 * BAND, LANES), 0)
    key_lane = lax.broadcasted_iota(jnp.int32, (4 * BAND, LANES), 1)
    den_cols = ((key_row < 2 * BAND) == (key_lane < ATTN_HEAD)).astype(BF16)

    def pv(pc, v2):
        return jnp.dot(pc, jnp.concatenate([v2, den_cols], axis=1), preferred_element_type=F32)

    rows4 = BAND // 4

    def unit_d16(r):
        def q2():
            return jnp.concatenate([qm_ref[0, r], qm_ref[1, r]], axis=0)

        def kb():
            return jnp.concatenate([kp_ref[0, 0, r], k_ref[0, 0, r]], axis=0)

        def v2():
            return jnp.concatenate([vm_ref[0, 1 - slot, r], vm_ref[0, slot, r],
                                    vm_ref[1, 1 - slot, r], vm_ref[1, slot, r]], axis=0)

        def store(acc, den, m_full):
            acc_ref[1, r] = acc
            den_ref[1, r] = den
            mx_ref[1, r] = m_full
        return 2, first_tile, q2, kb, v2, store

    def unit_d4(c, n):
        cur = slice(n * rows4, (n + 1) * rows4)
        prv = slice((n - 1) * rows4, n * rows4) if n else slice(BAND - rows4, BAND)
        slabs = [4 * a + c for a in range(4)]

        def q2():
            return jnp.concatenate([qm_ref[hh, sb, cur] for hh in range(2) for sb in slabs], axis=0)

        def kb():
            src = k_ref if n else kp_ref
            return jnp.concatenate([src[0, 0, sb, prv] for sb in slabs] + [k_ref[0, 0, sb, cur] for sb in slabs], axis=0)

        def v2():
            prev_slot = slot if n else 1 - slot
            return jnp.concatenate(
                [vm_ref[hh, prev_slot, sb, prv] if part == 0 else vm_ref[hh, slot, sb, cur]
                 for hh in range(2) for part in range(2) for sb in slabs], axis=0)

        def store(acc, den, m_full):
            for a, sb in enumerate(slabs):
                piece = slice(a * rows4, (a + 1) * rows4)
                acc_ref[0, sb, cur] = acc[piece]
                den_ref[0, sb, cur] = den[piece]
                mx_ref[0, sb, cur] = m_full[piece]
        return 1, (first_tile if n == 0 else 0), q2, kb, v2, store

    def unit_d1(n):
        q_rows = slice(n * SUBLANES, (n + 1) * SUBLANES)
        k_rows = slice(n * SUBLANES, (n + 2) * SUBLANES)
        tok = n * BAND

        def q2():
            return jnp.concatenate(split_heads(qf_ref[:, q_rows, :].reshape(BAND, LANES)), axis=0).astype(BF16)

        def kb():
            return kf_ref[:, k_rows, :].reshape(2 * BAND, LANES).astype(BF16)

        def v2():
            return jnp.concatenate(split_heads(vf_ref[:, k_rows, :].reshape(2 * BAND, LANES)), axis=0).astype(BF16)

        def store(acc, den, m_full):
            others = [tuple(ref[p, :, q_rows, :].reshape(BAND, LANES) for ref in (acc_ref, den_ref, mx_ref))
                      for p in range(2)]
            m_all = jnp.maximum(jnp.maximum(m_full, others[0][2]), others[1][2])
            w = jnp.exp2(m_full - m_all)
            num = w * acc
            total = w * den
            for acc_p, den_p, m_p in others:
                w = jnp.exp2(m_p - m_all)
                num = num + w * acc_p
                total = total + w * den_p
            y = (num / total).reshape(RES, SUBLANES, LANES)
            for r in range(RES):
                nat_ref[pl.ds(tok + r, SUBLANES, stride=RES), :] = y[r]
            rows = slice(tok, tok + BAND)
            gate = _silu(g_ref[0, rows, :].astype(F32))
            o_ref[0, rows, :] = (nat_ref[rows, :] * gate).astype(BF16)
        return 0, (first_tile if n == 0 else 0), q2, kb, v2, store

    units = [unit_d16(r) for r in range(RES)]
    d1_ready = []
    for n4 in range(4):
        for c in range(4):
            units.append(unit_d4(c, n4))
            if d1_ready:
                units.append(unit_d1(d1_ready.pop(0)))
        d1_ready.extend(range(4 * n4, 4 * n4 + 4))
    units.extend(unit_d1(n) for n in d1_ready)

    s, pm = {}, {}
    sm_lag, pv_lag = 1, 3
    for i in range(len(units) + pv_lag):
        if i == 4:
            unpack_for_d1()
        if i < len(units):
            s[i] = _scores(units[i][2](), units[i][3]())
        if 0 <= i - sm_lag < len(units):
            p, var = units[i - sm_lag][:2]
            pm[i - sm_lag] = _probs(s.pop(i - sm_lag), bias_ref[p, 0, var], bias_ref[p, 1, var], head0)
        if 0 <= i - pv_lag < len(units):
            pc, m_full = pm.pop(i - pv_lag)
            od = pv(pc, units[i - pv_lag][4]())
            units[i - pv_lag][5](od[:, :LANES], od[:, LANES:], m_full)


def _attn(za, zh, slopes, *, batch, seq):
    nt_ = seq // ATTN_TILE
    n_pat = len(DILATED_PATTERNS)
    blk = (1, 1, RES, BAND, LANES)

    def zspec(cb):
        return pl.BlockSpec(blk, lambda b, h, t, sl: (cb + h, b * nt_ + t, 0, 0, 0))

    prev_keys = pl.BlockSpec(blk, lambda b, h, t, sl: (CB_AK + h, b * nt_ + jnp.maximum(t - 1, 0), 0, 0, 0))
    gate = pl.BlockSpec((1, ATTN_TILE, LANES), lambda b, h, t, sl: (CB_AG + h, b * nt_ + t, 0))
    grid_spec = pltpu.PrefetchScalarGridSpec(
        num_scalar_prefetch=1,
        grid=(batch, HEAD_PAIRS, nt_),
        in_specs=[zspec(CB_AQ), zspec(CB_AK), zspec(CB_AV), prev_keys, gate],
        out_specs=pl.BlockSpec((1, ATTN_TILE, LANES), lambda b, h, t, sl: (h, b * nt_ + t, 0)),
        scratch_shapes=[
            pltpu.VMEM((n_pat, 2, 2, BAND, 2 * BAND), F32),
            pltpu.VMEM((2, RES, BAND, LANES), BF16),
            pltpu.VMEM((2, 2, RES, BAND, LANES), BF16),
            pltpu.VMEM((RES, BAND, LANES), F32),
            pltpu.VMEM((RES, SUBLANES + BAND, LANES), F32),
            pltpu.VMEM((RES, SUBLANES + BAND, LANES), F32),
            pltpu.VMEM((n_pat - 1, RES, BAND, LANES), F32),
            pltpu.VMEM((n_pat - 1, RES, BAND, LANES), F32),
            pltpu.VMEM((n_pat - 1, RES, BAND, LANES), F32),
            pltpu.VMEM((ATTN_TILE, LANES), F32),
        ],
    )
    return pl.pallas_call(
        _attn_kernel,
        grid_spec=grid_spec,
        out_shape=jax.ShapeDtypeStruct((HEAD_PAIRS, batch * seq, LANES), BF16),
        compiler_params=pltpu.CompilerParams(
            dimension_semantics=("parallel", "parallel", "arbitrary"), vmem_limit_bytes=VMEM_LIMIT),
        name="dilated_attn",
    )(slopes, za, za, za, za, zh)


def _outproj_kernel(yh_ref, ya_ref, x_ref, w_ref, g_ref, o_ref):
    y = jnp.concatenate([yh_ref[c] for c in range(yh_ref.shape[0])]
                        + [ya_ref[c] for c in range(ya_ref.shape[0])], axis=-1)
    r = x_ref[...] + jnp.dot(y, w_ref[...], preferred_element_type=F32)
    ms = jnp.mean(r * r, axis=-1, keepdims=True)
    o_ref[...] = r * lax.rsqrt(ms + NORM_EPS) * g_ref[...]


def _outproj(yh3, ya3, x2, w_bf16, gain, *, tm):
    m, d = x2.shape
    return pl.pallas_call(
        _outproj_kernel,
        grid=(m // tm,),
        in_specs=[
            pl.BlockSpec((yh3.shape[0], tm, LANES), lambda i: (0, i, 0)),
            pl.BlockSpec((ya3.shape[0], tm, LANES), lambda i: (0, i, 0)),
            pl.BlockSpec((tm, d), lambda i: (i, 0)),
            pl.BlockSpec(w_bf16.shape, lambda i: (0, 0)),
            pl.BlockSpec((1, d), lambda i: (0, 0)),
        ],
        out_specs=pl.BlockSpec((tm, d), lambda i: (i, 0)),
        out_shape=jax.ShapeDtypeStruct((m, d), F32),
        compiler_params=pltpu.CompilerParams(
            dimension_semantics=("parallel",), vmem_limit_bytes=VMEM_LIMIT),
        name="outproj",
    )(yh3, ya3, x2, w_bf16, gain)


def kernel(x, norm_gain, w_in, lb_logits, hgrn_gnorm, w_out, final_gain):
    batch, seq, d = x.shape
    assert d == D_MODEL and w_in.shape == (1, D_MODEL, 4 * D_HGRN + 4 * D_ATTN) and lb_logits.shape[0] == 2
    m = batch * seq
    assert seq % ATTN_TILE == 0 and seq % HGRN_TILE == 0 and m % PROJ_ROWS == 0 and m % OUT_ROWS == 0
    assert D_HGRN == D_ATTN == TOKEN_PROJ_COLS

    x2 = x.reshape(m, d)
    w_in_bf16 = w_in[0].astype(BF16)
    gain = norm_gain[0:1]
    zh = _inproj(x2, gain, w_in_bf16, tm=PROJ_ROWS, tn=TOKEN_PROJ_COLS, n_col_tiles=5,
                 col_tile=lambda j: jnp.where(j < 4, j, 7), by_residue=False)
    w_qkv = w_in_bf16[:, 4 * D_HGRN:4 * D_HGRN + 3 * D_ATTN]
    za, yh3 = _inproj(x2, gain, w_qkv, tm=PROJ_ROWS, tn=RESIDUE_PROJ_COLS,
                      n_col_tiles=3 * D_ATTN // RESIDUE_PROJ_COLS, col_tile=lambda j: j, by_residue=True,
                      hgrn=(zh, lb_logits, hgrn_gnorm[0:1], batch, seq, HGRN_TILE))
    slopes = jnp.exp2(-8.0 * jnp.arange(1, ATTN_HEADS + 1, dtype=F32) / ATTN_HEADS)
    ya3 = _attn(za, zh, slopes, batch=batch, seq=seq)
    out = _outproj(yh3, ya3, x2, w_out[0].astype(BF16), final_gain.reshape(1, d), tm=OUT_ROWS)
    return out.reshape(batch, seq, d)
```

```python
import functools

import jax
import jax.numpy as jnp
from jax import lax
from jax.experimental import pallas as pl
from jax.experimental.pallas import tpu as pltpu

D_MODEL = 2048
D_HGRN = 1024
HGRN_HEAD = 128
HGRN_HEADS = D_HGRN // HGRN_HEAD
HGRN_CHUNK = 64
D_ATTN = 1024
ATTN_HEAD = 64
ATTN_HEADS = D_ATTN // ATTN_HEAD
DILATED_PATTERNS = ((128, 1), (512, 4), (2048, 16))
NORM_EPS = 1e-6

LANES = 128
SUBLANES = 8
BAND = 128
HEAD_PAIRS = ATTN_HEADS // 2
MASK_VALUE = -1e30
LOG2_E = 1.4426950408889634
RES = 16
ATTN_TILE = BAND * RES

CB_HQ, CB_HF, CB_HI, CB_HG, CB_AG = 0, 8, 16, 24, 32
CB_AQ, CB_AK, CB_AV = 0, 8, 16

VMEM_LIMIT = 56 * 1024 * 1024

PROJ_ROWS = 1024
TOKEN_PROJ_COLS = 1024
RESIDUE_PROJ_COLS = 3 * D_ATTN // 4
HGRN_TILE = PROJ_ROWS
HGRN_HEADS_PER_STEP = 2
OUT_ROWS = 512

F32 = jnp.float32
BF16 = jnp.bfloat16
NT_DIMS = (((1,), (1,)), ((), ()))
TN_DIMS = (((0,), (0,)), ((), ()))


def _silu(v):
    u = 0.5 * v
    return u * jnp.tanh(u) + u


def _interleave(pieces, phases, advance_after):
    for _ in range(advance_after.get(-1, 0)):
        next(phases, None)
    for k, piece in enumerate(pieces):
        piece()
        for _ in range(advance_after.get(k, 0)):
            next(phases, None)
    for _ in phases:
        pass


def _inproj_kernel(x_ref, *rest, by_residue, q_cols, q_scale, hgrn_steps_per_seq):
    if by_residue:
        (g_ref, w_ref, zq_ref, zf_ref, zi_ref, zg_ref, lbl_ref, gn_ref, o_ref, yo_ref,
         h_ref, xs_ref, y_ref, st_ref) = rest
    else:
        g_ref, w_ref, o_ref, h_ref = rest
    tm, d = h_ref.shape
    n_lane_blocks = d // LANES
    tn = w_ref.shape[1]
    per_res = tm // RES
    chunk = tm // 4
    i = pl.program_id(0)
    j = pl.program_id(1)
    sub = 2 * LANES

    def project(c, h_rows, row0, scale):
        n_rows = h_rows.shape[0]
        acc = jnp.dot(h_rows, w_ref[:, c * sub:(c + 1) * sub], preferred_element_type=F32)
        if scale is not None:
            acc = acc * scale
        for half in range(2):
            part = acc[:, half * LANES:(half + 1) * LANES].astype(BF16)
            if by_residue:
                for k in range(n_rows // per_res):
                    blk = row0 // per_res + k
                    o_ref[2 * c + half, 0, 4 * (blk % 4) + blk // 4] = part[k * per_res:(k + 1) * per_res]
            else:
                o_ref[2 * c + half, row0:row0 + n_rows] = part

    def normalise_quarter(c4):
        rows = slice(c4 * chunk, (c4 + 1) * chunk)
        if by_residue:
            if c4 == 0:
                for c in range(n_lane_blocks):
                    xs_ref[c] = x_ref[:, c * LANES:(c + 1) * LANES]
            for c in range(n_lane_blocks):
                y_ref[c] = xs_ref[c, pl.ds(c4, chunk, stride=4), :]
            for a in range(4):
                src = pl.ds(a, per_res, stride=4)
                dst = slice((4 * c4 + a) * per_res, (4 * c4 + a + 1) * per_res)
                cols = [y_ref[c, src, :] for c in range(n_lane_blocks)]
                ssq = cols[0] * cols[0]
                for xc in cols[1:]:
                    ssq = ssq + xc * xc
                scale = lax.rsqrt(jnp.sum(ssq, axis=-1, keepdims=True) * (1.0 / d) + NORM_EPS)
                for c, xc in enumerate(cols):
                    lanes = slice(c * LANES, (c + 1) * LANES)
                    h_ref[dst, lanes] = (xc * scale * g_ref[:, lanes]).astype(BF16)
        else:
            xv = x_ref[rows, :]
            ms = jnp.mean(xv * xv, axis=-1, keepdims=True)
            h_ref[rows, :] = (xv * lax.rsqrt(ms + NORM_EPS) * g_ref[...]).astype(BF16)

    def hgrn():
        step = i * pl.num_programs(1) + j
        first = lax.rem(step, hgrn_steps_per_seq) == 0
        return _hgrn_phases(zq_ref, zf_ref, zi_ref, zg_ref, lbl_ref, gn_ref, yo_ref, st_ref, first)

    def run(pieces, slots):
        if by_residue:
            _interleave(pieces, hgrn(), {-1: 1, **{slot: 2 for slot in slots}})
        else:
            for piece in pieces:
                piece()

    @pl.when(j == 0)
    def _():
        pieces = []
        for c4 in range(4):
            rows = slice(c4 * chunk, (c4 + 1) * chunk)
            for c in range(tn // sub):
                scale = q_scale if c * sub < q_cols else None

                def piece(c4=c4, c=c, rows=rows, scale=scale):
                    if c == 0:
                        normalise_quarter(c4)
                    project(c, h_ref[rows, :], rows.start, scale)
                pieces.append(piece)
        per_quarter = tn // sub
        run(pieces, [k * per_quarter - 1 for k in (1, 2, 3)])

    @pl.when(j > 0)
    def _():
        pieces = []
        for c in range(tn // sub):
            n_q_tiles = -(-(q_cols - c * sub) // tn) if q_cols > c * sub else 0
            scale = jnp.where(j < n_q_tiles, q_scale, 1.0) if n_q_tiles > 1 else None
            pieces.append(lambda c=c, scale=scale: project(c, h_ref[...], 0, scale))
        run(pieces, range(3))


def _inproj(x2, gain, w_bf16, *, tm, tn, n_col_tiles, col_start, by_residue, hgrn=None):
    m, d = x2.shape
    n = n_col_tiles * tn
    grid = (m // tm, n_col_tiles)
    scratch = [pltpu.VMEM((tm, d), BF16)]
    extra_in, extra_specs, out_shapes, out_specs = [], [], [], []
    if by_residue:
        halves = ATTN_TILE // tm
        out_shapes.append(jax.ShapeDtypeStruct((n // LANES, m // ATTN_TILE, RES, BAND, LANES), BF16))
        out_specs.append(pl.BlockSpec((tn // LANES, 1, RES, tm // RES, LANES),
                                      lambda i, j: (j, i // halves, 0, i % halves, 0)))
        scratch.append(pltpu.VMEM((d // LANES, tm, LANES), F32))
        scratch.append(pltpu.VMEM((d // LANES, tm // 4, LANES), F32))

        zh, lb_logits, gnorm, batch, seq, ts = hgrn
        nh = HGRN_HEADS_PER_STEP
        pairs, seq_tiles = HGRN_HEADS // nh, seq // ts
        assert grid[0] * grid[1] == batch * pairs * seq_tiles

        def where(i, j):
            step = i * n_col_tiles + j
            return (step // seq_tiles) % pairs, (step // (pairs * seq_tiles)) * seq_tiles + step % seq_tiles

        def zspec(cb):
            return pl.BlockSpec((nh, ts, LANES), lambda i, j: (cb // nh + where(i, j)[0], where(i, j)[1], 0))

        extra_in = [zh, zh, zh, zh, lb_logits, gnorm]
        extra_specs = [zspec(CB_HQ), zspec(CB_HF), zspec(CB_HI), zspec(CB_HG),
                       pl.BlockSpec((lb_logits.shape[0], nh * LANES), lambda i, j: (0, where(i, j)[0])),
                       pl.BlockSpec((1, LANES), lambda i, j: (0, 0))]
        out_shapes.append(jax.ShapeDtypeStruct((HGRN_HEADS, batch * seq, LANES), BF16))
        out_specs.append(pl.BlockSpec((nh, ts, LANES), lambda i, j: (where(i, j)[0], where(i, j)[1], 0)))
        scratch.append(pltpu.VMEM((nh, HGRN_HEAD, HGRN_HEAD), F32))
        steps_per_seq = seq_tiles
    else:
        out_shapes.append(jax.ShapeDtypeStruct((n // LANES, m, LANES), BF16))
        out_specs.append(pl.BlockSpec((tn // LANES, tm, LANES), lambda i, j: (j, i, 0)))
        steps_per_seq = None
    outs = pl.pallas_call(
        functools.partial(_inproj_kernel, by_residue=by_residue,
                          q_cols=D_ATTN if by_residue else 0, q_scale=ATTN_HEAD ** -0.5 * LOG2_E,
                          hgrn_steps_per_seq=steps_per_seq),
        grid=grid,
        in_specs=[
            pl.BlockSpec((tm, d), lambda i, j: (i, 0)),
            pl.BlockSpec((1, d), lambda i, j: (0, 0)),
            pl.BlockSpec((pl.Element(d), pl.Element(tn)), lambda i, j: (0, pl.multiple_of(col_start(j), 2 * LANES))),
        ] + extra_specs,
        out_specs=out_specs,
        out_shape=out_shapes,
        scratch_shapes=scratch,
        compiler_params=pltpu.CompilerParams(
            dimension_semantics=("arbitrary", "arbitrary") if by_residue else ("parallel", "arbitrary"),
            vmem_limit_bytes=VMEM_LIMIT),
        name="inproj_attn_hgrn2" if by_residue else "inproj_token",
    )(x2, gain, w_bf16, *extra_in)
    return outs if by_residue else outs[0]


def _hgrn_phases(q_ref, f_ref, i_ref, g_ref, lbl_ref, gn_ref, o_ref, st_ref, first_of_sequence):
    nh, ts, _ = q_ref.shape
    heads = range(nh)
    c_len = HGRN_CHUNK
    chunks = [slice(c * c_len, (c + 1) * c_len) for c in range(ts // c_len)]
    row = lax.broadcasted_iota(jnp.int32, (c_len, c_len), 0)
    col = lax.broadcasted_iota(jnp.int32, (c_len, c_len), 1)
    causal = col <= row
    tril = causal.astype(BF16)

    lbl = lbl_ref[...]
    e = jnp.exp(lbl - jnp.max(lbl, axis=0, keepdims=True))
    lb_all = e[0:1, :] / jnp.sum(e, axis=0, keepdims=True)
    lb = [lb_all[:, h * LANES:(h + 1) * LANES] for h in heads]

    q = [_silu(q_ref[h].astype(F32)) for h in heads]
    half_range = [0.5 * (1.0 - lb[h]) for h in heads]
    mid = [lb[h] + half_range[h] for h in heads]
    f = [mid[h] + half_range[h] * jnp.tanh(0.5 * f_ref[h].astype(F32)) for h in heads]
    k = [1.0 - f[h] for h in heads]
    log_f = [jnp.log2(f[h]) for h in heads]
    hilo = []
    for h in heads:
        hi = log_f[h].astype(BF16)
        lo = (log_f[h] - hi.astype(F32)).astype(BF16)
        hilo.append(jnp.concatenate([hi, lo], axis=-1))
    yield
    cs = [jnp.concatenate([jnp.dot(tril, hilo[h][sl], preferred_element_type=F32) for sl in chunks], axis=0)
          for h in heads]
    yield
    b_cum =[cs[h][:, :LANES] + cs[h][:, LANES:] for h in heads]
    decay = [[jnp.exp2(b_cum[h][sl.stop - 1:sl.stop, :]) for sl in chunks] for h in heads]
    q_dec = [(q[h] * jnp.exp2(b_cum[h])).astype(BF16) for h in heads]
    k_dec_f = [k[h] * jnp.exp2(-b_cum[h]) for h in heads]
    k_dec = [k_dec_f[h].astype(BF16) for h in heads]
    k_end = [jnp.concatenate([k_dec_f[h][sl] * decay[h][c] for c, sl in enumerate(chunks)], axis=0).astype(BF16)
             for h in heads]
    v = [i_ref[h] for h in heads]
    yield
    kv_t = [[lax.dot_general(v[h][sl], k_end[h][sl], TN_DIMS, preferred_element_type=F32) for sl in chunks]
            for h in heads]
    a = [[lax.dot_general(q_dec[h][sl], k_dec[h][sl], NT_DIMS, preferred_element_type=F32) for sl in chunks]
         for h in heads]
    yield
    a = [[jnp.where(causal, a_c, 0.0).astype(BF16) for a_c in a[h]] for h in heads]
    entering = [[] for _ in heads]
    for h in heads:
        st = jnp.where(first_of_sequence, 0.0, st_ref[h])
        for c in range(len(chunks)):
            entering[h].append(st.astype(BF16))
            st = st * decay[h][c] + kv_t[h][c]
        st_ref[h] = st
    yield
    o = [jnp.concatenate(
        [jnp.dot(a[h][c], v[h][sl], preferred_element_type=F32)
         + lax.dot_general(q_dec[h][sl], entering[h][c], NT_DIMS, preferred_element_type=F32)
         for c, sl in enumerate(chunks)], axis=0) for h in heads]
    yield
    gn = gn_ref[...]
    for h in heads:
        o_h = o[h] * lax.rsqrt(jnp.mean(o[h] * o[h], axis=-1, keepdims=True) + NORM_EPS) * gn
        o_ref[h] = (o_h * _silu(g_ref[h].astype(F32))).astype(BF16)


def _unit_positions(p):
    rho = lax.broadcasted_iota(jnp.int32, (BAND, 2 * BAND), 0)
    kap = lax.broadcasted_iota(jnp.int32, (BAND, 2 * BAND), 1)
    if p == 2:
        return BAND + rho, kap
    if p == 1:
        q_pos = BAND + 4 * (rho & 31) + (rho >> 5)
        k_pos = (kap >> 7) * BAND + 4 * (kap & 31) + ((kap >> 5) & 3)
        return q_pos, k_pos
    q_pos = BAND + RES * (rho & 7) + (rho >> 3)
    k_pos = RES * (kap & 15) + (kap >> 4)
    return q_pos, k_pos


def _scores(q2, kb):
    return lax.dot_general(q2, kb, NT_DIMS, preferred_element_type=F32)


def _probs(s, bias0, bias1, head0):
    s0 = s[:BAND] + bias0
    m0 = jnp.max(s0, axis=-1, keepdims=True)
    p0 = jnp.exp2(s0 - m0).astype(BF16)
    s1 = s[BAND:] + bias1
    m1 = jnp.max(s1, axis=-1, keepdims=True)
    p1 = jnp.exp2(s1 - m1).astype(BF16)
    return jnp.concatenate([p0, p1], axis=1), jnp.where(head0, m0, m1)


def _attn_kernel(slopes_ref, q_ref, k_ref, v_ref, kp_ref, g_ref, o_ref,
                 bias_ref, qm_ref, vm_ref, qf_ref, kf_ref, vf_ref, acc_ref, mx_ref, den_ref, nat_ref):
    hp = pl.program_id(1)
    tile = pl.program_id(2)
    first_tile = jnp.where(tile == 0, 1, 0)
    slot = lax.bitwise_and(tile, 1)
    lane = lax.broadcasted_iota(jnp.int32, (1, LANES), 1)
    head0 = lane < ATTN_HEAD
    prev_rows = SUBLANES

    @pl.when(tile == 0)
    def _():
        vm_ref[0, 1] = jnp.zeros((RES, BAND, LANES), BF16)
        vm_ref[1, 1] = jnp.zeros((RES, BAND, LANES), BF16)
        for p, (_, dil) in enumerate(DILATED_PATTERNS):
            q_pos, k_pos = _unit_positions(p)
            delta = q_pos - k_pos
            in_band = (delta >= 0) & (delta <= BAND)
            not_prev = k_pos >= BAND
            delta_f = delta.astype(F32)
            for hh in range(2):
                slope = slopes_ref[2 * hp + hh]
                bias = -(slope * (float(dil) * LOG2_E)) * delta_f
                bias_ref[p, hh, 0] = jnp.where(in_band, bias, MASK_VALUE)
                bias_ref[p, hh, 1] = jnp.where(in_band & not_prev, bias, MASK_VALUE)

    def split_heads(x):
        zero = jnp.zeros_like(x)
        return jnp.where(head0, x, zero), jnp.where(head0, zero, x)

    qm_ref[0], qm_ref[1] = split_heads(q_ref[0, 0])
    vm_ref[0, slot], vm_ref[1, slot] = split_heads(v_ref[0, 0])

    def unpack_for_d1():
        qf_ref[...] = q_ref[0, 0].astype(F32)
        packed_rows = 2 * SUBLANES
        v_tail = (vm_ref[0, 1 - slot, :, BAND - packed_rows:, :].astype(F32)
                  + vm_ref[1, 1 - slot, :, BAND - packed_rows:, :].astype(F32))
        vf_ref[:, :prev_rows, :] = v_tail[:, packed_rows - prev_rows:, :]
        vf_ref[:, prev_rows:, :] = v_ref[0, 0].astype(F32)
        k_tail = kp_ref[0, 0, :, BAND - packed_rows:, :].astype(F32)
        kf_ref[:, :prev_rows, :] = k_tail[:, packed_rows - prev_rows:, :]
        kf_ref[:, prev_rows:, :] = k_ref[0, 0].astype(F32)

    key_row = lax.broadcasted_iota(jnp.int32, (4 * BAND, LANES), 0)
    key_lane = lax.broadcasted_iota(jnp.int32, (4 * BAND, LANES), 1)
    den_cols = ((key_row < 2 * BAND) == (key_lane < ATTN_HEAD)).astype(BF16)

    def pv(pc, v2):
        return jnp.dot(pc, jnp.concatenate([v2, den_cols], axis=1), preferred_element_type=F32)

    rows4 = BAND // 4

    def unit_d16(r):
        def q2():
            return jnp.concatenate([qm_ref[0, r], qm_ref[1, r]], axis=0)

        def kb():
            return jnp.concatenate([kp_ref[0, 0, r], k_ref[0, 0, r]], axis=0)

        def v2():
            return jnp.concatenate([vm_ref[0, 1 - slot, r], vm_ref[0, slot, r],
                                    vm_ref[1, 1 - slot, r], vm_ref[1, slot, r]], axis=0)

        def store(acc, den, m_full):
            acc_ref[1, r] = acc
            den_ref[1, r] = den
            mx_ref[1, r] = m_full
        return 2, first_tile, q2, kb, v2, store

    def unit_d4(c, n):
        cur = slice(n * rows4, (n + 1) * rows4)
        prv = slice((n - 1) * rows4, n * rows4) if n else slice(BAND - rows4, BAND)
        slabs = [4 * a + c for a in range(4)]

        def q2():
            return jnp.concatenate([qm_ref[hh, sb, cur] for hh in range(2) for sb in slabs], axis=0)

        def kb():
            src = k_ref if n else kp_ref
            return jnp.concatenate([src[0, 0, sb, prv] for sb in slabs] + [k_ref[0, 0, sb, cur] for sb in slabs], axis=0)

        def v2():
            prev_slot = slot if n else 1 - slot
            return jnp.concatenate(
                [vm_ref[hh, prev_slot, sb, prv] if part == 0 else vm_ref[hh, slot, sb, cur]
                 for hh in range(2) for part in range(2) for sb in slabs], axis=0)

        def store(acc, den, m_full):
            for a, sb in enumerate(slabs):
                piece = slice(a * rows4, (a + 1) * rows4)
                acc_ref[0, sb, cur] = acc[piece]
                den_ref[0, sb, cur] = den[piece]
                mx_ref[0, sb, cur] = m_full[piece]
        return 1, (first_tile if n == 0 else 0), q2, kb, v2, store

    def unit_d1(n):
        q_rows = slice(n * SUBLANES, (n + 1) * SUBLANES)
        k_rows = slice(n * SUBLANES, (n + 2) * SUBLANES)
        tok = n * BAND

        def q2():
            return jnp.concatenate(split_heads(qf_ref[:, q_rows, :].reshape(BAND, LANES)), axis=0).astype(BF16)

        def kb():
            return kf_ref[:, k_rows, :].reshape(2 * BAND, LANES).astype(BF16)

        def v2():
            return jnp.concatenate(split_heads(vf_ref[:, k_rows, :].reshape(2 * BAND, LANES)), axis=0).astype(BF16)

        def store(acc, den, m_full):
            others = [tuple(ref[p, :, q_rows, :].reshape(BAND, LANES) for ref in (acc_ref, den_ref, mx_ref))
                      for p in range(2)]
            m_all = jnp.maximum(jnp.maximum(m_full, others[0][2]), others[1][2])
            w = jnp.exp2(m_full - m_all)
            num = w * acc
            total = w * den
            for acc_p, den_p, m_p in others:
                w = jnp.exp2(m_p - m_all)
                num = num + w * acc_p
                total = total + w * den_p
            y = (num / total).reshape(RES, SUBLANES, LANES)
            for r in range(RES):
                nat_ref[pl.ds(tok + r, SUBLANES, stride=RES), :] = y[r]
            rows = slice(tok, tok + BAND)
            gate = _silu(g_ref[0, rows, :].astype(F32))
            o_ref[0, rows, :] = (nat_ref[rows, :] * gate).astype(BF16)
        return 0, (first_tile if n == 0 else 0), q2, kb, v2, store

    units = [unit_d16(r) for r in range(RES)]
    d1_ready = []
    for n4 in range(4):
        for c in range(4):
            units.append(unit_d4(c, n4))
            if d1_ready:
                units.append(unit_d1(d1_ready.pop(0)))
        d1_ready.extend(range(4 * n4, 4 * n4 + 4))
    units.extend(unit_d1(n) for n in d1_ready)

    s, pm = {}, {}
    sm_lag, pv_lag = 1, 3
    for i in range(len(units) + pv_lag):
        if i == 4:
            unpack_for_d1()
        if i < len(units):
            s[i] = _scores(units[i][2](), units[i][3]())
        if 0 <= i - sm_lag < len(units):
            p, var = units[i - sm_lag][:2]
            pm[i - sm_lag] = _probs(s.pop(i - sm_lag), bias_ref[p, 0, var], bias_ref[p, 1, var], head0)
        if 0 <= i - pv_lag < len(units):
            pc, m_full = pm.pop(i - pv_lag)
            od = pv(pc, units[i - pv_lag][4]())
            units[i - pv_lag][5](od[:, :LANES], od[:, LANES:], m_full)


def _attn(za, zh, slopes, *, batch, seq):
    nt_ = seq // ATTN_TILE
    n_pat = len(DILATED_PATTERNS)
    blk = (1, 1, RES, BAND, LANES)

    def zspec(cb):
        return pl.BlockSpec(blk, lambda b, h, t, sl: (cb + h, b * nt_ + t, 0, 0, 0))

    prev_keys = pl.BlockSpec(blk, lambda b, h, t, sl: (CB_AK + h, b * nt_ + jnp.maximum(t - 1, 0), 0, 0, 0))
    gate = pl.BlockSpec((1, ATTN_TILE, LANES), lambda b, h, t, sl: (CB_AG + h, b * nt_ + t, 0))
    grid_spec = pltpu.PrefetchScalarGridSpec(
        num_scalar_prefetch=1,
        grid=(batch, HEAD_PAIRS, nt_),
        in_specs=[zspec(CB_AQ), zspec(CB_AK), zspec(CB_AV), prev_keys, gate],
        out_specs=pl.BlockSpec((1, ATTN_TILE, LANES), lambda b, h, t, sl: (h, b * nt_ + t, 0)),
        scratch_shapes=[
            pltpu.VMEM((n_pat, 2, 2, BAND, 2 * BAND), F32),
            pltpu.VMEM((2, RES, BAND, LANES), BF16),
            pltpu.VMEM((2, 2, RES, BAND, LANES), BF16),
            pltpu.VMEM((RES, BAND, LANES), F32),
            pltpu.VMEM((RES, SUBLANES + BAND, LANES), F32),
            pltpu.VMEM((RES, SUBLANES + BAND, LANES), F32),
            pltpu.VMEM((n_pat - 1, RES, BAND, LANES), F32),
            pltpu.VMEM((n_pat - 1, RES, BAND, LANES), F32),
            pltpu.VMEM((n_pat - 1, RES, BAND, LANES), F32),
            pltpu.VMEM((ATTN_TILE, LANES), F32),
        ],
    )
    return pl.pallas_call(
        _attn_kernel,
        grid_spec=grid_spec,
        out_shape=jax.ShapeDtypeStruct((HEAD_PAIRS, batch * seq, LANES), BF16),
        compiler_params=pltpu.CompilerParams(
            dimension_semantics=("parallel", "parallel", "arbitrary"), vmem_limit_bytes=VMEM_LIMIT),
        name="dilated_attn",
    )(slopes, za, za, za, za, zh)


def _outproj_kernel(yh_ref, ya_ref, x_ref, w_ref, g_ref, o_ref):
    y = jnp.concatenate([yh_ref[c] for c in range(yh_ref.shape[0])]
                        + [ya_ref[c] for c in range(ya_ref.shape[0])], axis=-1)
    r = x_ref[...] + jnp.dot(y, w_ref[...], preferred_element_type=F32)
    ms = jnp.mean(r * r, axis=-1, keepdims=True)
    o_ref[...] = r * lax.rsqrt(ms + NORM_EPS) * g_ref[...]


def _outproj(yh3, ya3, x2, w_bf16, gain, *, tm):
    m, d = x2.shape
    return pl.pallas_call(
        _outproj_kernel,
        grid=(m // tm,),
        in_specs=[
            pl.BlockSpec((yh3.shape[0], tm, LANES), lambda i: (0, i, 0)),
            pl.BlockSpec((ya3.shape[0], tm, LANES), lambda i: (0, i, 0)),
            pl.BlockSpec((tm, d), lambda i: (i, 0)),
            pl.BlockSpec(w_bf16.shape, lambda i: (0, 0)),
            pl.BlockSpec((1, d), lambda i: (0, 0)),
        ],
        out_specs=pl.BlockSpec((tm, d), lambda i: (i, 0)),
        out_shape=jax.ShapeDtypeStruct((m, d), F32),
        compiler_params=pltpu.CompilerParams(
            dimension_semantics=("parallel",), vmem_limit_bytes=VMEM_LIMIT),
        name="outproj",
    )(yh3, ya3, x2, w_bf16, gain)


def kernel(x, norm_gain, w_in, lb_logits, hgrn_gnorm, w_out, final_gain):
    batch, seq, d = x.shape
    assert d == D_MODEL and w_in.shape == (1, D_MODEL, 4 * D_HGRN + 4 * D_ATTN) and lb_logits.shape[0] == 2
    m = batch * seq
    assert seq % ATTN_TILE == 0 and seq % HGRN_TILE == 0 and m % PROJ_ROWS == 0 and m % OUT_ROWS == 0
    assert D_HGRN == D_ATTN == TOKEN_PROJ_COLS

    x2 = x.reshape(m, d)
    w_in_bf16 = w_in[0].astype(BF16)
    gain = norm_gain[0:1]
    zh = _inproj(x2, gain, w_in_bf16, tm=PROJ_ROWS, tn=TOKEN_PROJ_COLS, n_col_tiles=5,
                 col_start=lambda j: jnp.where(j < 4, j, 7) * TOKEN_PROJ_COLS, by_residue=False)
    za, yh3 = _inproj(x2, gain, w_in_bf16, tm=PROJ_ROWS, tn=RESIDUE_PROJ_COLS,
                      n_col_tiles=3 * D_ATTN // RESIDUE_PROJ_COLS,
                      col_start=lambda j: 4 * D_HGRN + j * RESIDUE_PROJ_COLS, by_residue=True,
                      hgrn=(zh, lb_logits, hgrn_gnorm[0:1], batch, seq, HGRN_TILE))
    slopes = jnp.exp2(-8.0 * jnp.arange(1, ATTN_HEADS + 1, dtype=F32) / ATTN_HEADS)
    ya3 = _attn(za, zh, slopes, batch=batch, seq=seq)
    out = _outproj(yh3, ya3, x2, w_out[0].astype(BF16), final_gain.reshape(1, d), tm=OUT_ROWS)
    return out.reshape(batch, seq, d)
```

```python
import functools

import jax
import jax.numpy as jnp
from jax import lax
from jax.experimental import pallas as pl
from jax.experimental.pallas import tpu as pltpu

D_MODEL = 2048
D_HGRN = 1024
HGRN_HEAD = 128
HGRN_HEADS = D_HGRN // HGRN_HEAD
HGRN_CHUNK = 64
D_ATTN = 1024
ATTN_HEAD = 64
ATTN_HEADS = D_ATTN // ATTN_HEAD
DILATED_PATTERNS = ((128, 1), (512, 4), (2048, 16))
NORM_EPS = 1e-6

LANES = 128
SUBLANES = 8
BAND = 128
HEAD_PAIRS = ATTN_HEADS // 2
MASK_VALUE = -1e30
LOG2_E = 1.4426950408889634
RES = 16
ATTN_TILE = BAND * RES

CB_HQ, CB_HF, CB_HI, CB_HG, CB_AG = 0, 8, 16, 24, 32
CB_AQ, CB_AK, CB_AV = 0, 8, 16

VMEM_LIMIT = 56 * 1024 * 1024

PROJ_ROWS = 1024
TOKEN_PROJ_COLS = 1024
RESIDUE_PROJ_COLS = 3 * D_ATTN // 4
HGRN_TILE = PROJ_ROWS
HGRN_HEADS_PER_STEP = 2
OUT_ROWS = 512

F32 = jnp.float32
BF16 = jnp.bfloat16
NT_DIMS = (((1,), (1,)), ((), ()))
TN_DIMS = (((0,), (0,)), ((), ()))


def _silu(v):
    u = 0.5 * v
    return u * jnp.tanh(u) + u


def _interleave(pieces, phases, advance_after):
    for _ in range(advance_after.get(-1, 0)):
        next(phases, None)
    for k, piece in enumerate(pieces):
        piece()
        for _ in range(advance_after.get(k, 0)):
            next(phases, None)
    for _ in phases:
        pass


def _inproj_kernel(x_ref, *rest, by_residue, q_cols, q_scale, hgrn_steps_per_seq):
    if by_residue:
        (g_ref, w_ref, zq_ref, zf_ref, zi_ref, zg_ref, lbl_ref, gn_ref, o_ref, yo_ref,
         h_ref, xs_ref, y_ref, st_ref) = rest
    else:
        g_ref, w_ref, o_ref, h_ref = rest
    tm, d = h_ref.shape
    n_lane_blocks = d // LANES
    tn = w_ref.shape[1]
    per_res = tm // RES
    chunk = tm // 4
    i = pl.program_id(0)
    j = pl.program_id(1)
    sub = 2 * LANES

    def project(c, h_rows, row0, scale):
        n_rows = h_rows.shape[0]
        acc = jnp.dot(h_rows, w_ref[:, c * sub:(c + 1) * sub], preferred_element_type=F32)
        if scale is not None:
            acc = acc * scale
        for half in range(2):
            part = acc[:, half * LANES:(half + 1) * LANES].astype(BF16)
            if by_residue:
                for k in range(n_rows // per_res):
                    blk = row0 // per_res + k
                    o_ref[2 * c + half, 0, 4 * (blk % 4) + blk // 4] = part[k * per_res:(k + 1) * per_res]
            else:
                o_ref[2 * c + half, row0:row0 + n_rows] = part

    def normalise_quarter(c4):
        rows = slice(c4 * chunk, (c4 + 1) * chunk)
        if by_residue:
            if c4 == 0:
                for c in range(n_lane_blocks):
                    xs_ref[c] = x_ref[:, c * LANES:(c + 1) * LANES]
            for c in range(n_lane_blocks):
                y_ref[c] = xs_ref[c, pl.ds(c4, chunk, stride=4), :]
            for a in range(4):
                src = pl.ds(a, per_res, stride=4)
                dst = slice((4 * c4 + a) * per_res, (4 * c4 + a + 1) * per_res)
                cols = [y_ref[c, src, :] for c in range(n_lane_blocks)]
                ssq = cols[0] * cols[0]
                for xc in cols[1:]:
                    ssq = ssq + xc * xc
                scale = lax.rsqrt(jnp.sum(ssq, axis=-1, keepdims=True) * (1.0 / d) + NORM_EPS)
                for c, xc in enumerate(cols):
                    lanes = slice(c * LANES, (c + 1) * LANES)
                    h_ref[dst, lanes] = (xc * scale * g_ref[:, lanes]).astype(BF16)
        else:
            xv = x_ref[rows, :]
            ms = jnp.mean(xv * xv, axis=-1, keepdims=True)
            h_ref[rows, :] = (xv * lax.rsqrt(ms + NORM_EPS) * g_ref[...]).astype(BF16)

    def hgrn():
        step = i * pl.num_programs(1) + j
        first = lax.rem(step, hgrn_steps_per_seq) == 0
        return _hgrn_phases(zq_ref, zf_ref, zi_ref, zg_ref, lbl_ref, gn_ref, yo_ref, st_ref, first)

    def run(pieces, slots):
        if by_residue:
            _interleave(pieces, hgrn(), {-1: 1, **{slot: 2 for slot in slots}})
        else:
            for piece in pieces:
                piece()

    @pl.when(j == 0)
    def _():
        pieces = []
        for c4 in range(4):
            rows = slice(c4 * chunk, (c4 + 1) * chunk)
            for c in range(tn // sub):
                scale = q_scale if c * sub < q_cols else None

                def piece(c4=c4, c=c, rows=rows, scale=scale):
                    if c == 0:
                        normalise_quarter(c4)
                    project(c, h_ref[rows, :], rows.start, scale)
                pieces.append(piece)
        per_quarter = tn // sub
        run(pieces, [k * per_quarter - 1 for k in (1, 2, 3)])

    @pl.when(j > 0)
    def _():
        pieces = []
        for c in range(tn // sub):
            n_q_tiles = -(-(q_cols - c * sub) // tn) if q_cols > c * sub else 0
            scale = jnp.where(j < n_q_tiles, q_scale, 1.0) if n_q_tiles > 1 else None
            pieces.append(lambda c=c, scale=scale: project(c, h_ref[...], 0, scale))
        run(pieces, range(3))


def _inproj(x2, gain, w_bf16, *, tm, tn, n_col_tiles, col_start, by_residue, hgrn=None):
    m, d = x2.shape
    n = n_col_tiles * tn
    grid = (m // tm, n_col_tiles)
    scratch = [pltpu.VMEM((tm, d), BF16)]
    extra_in, extra_specs, out_shapes, out_specs = [], [], [], []
    if by_residue:
        halves = ATTN_TILE // tm
        out_shapes.append(jax.ShapeDtypeStruct((n // LANES, m // ATTN_TILE, RES, BAND, LANES), BF16))
        out_specs.append(pl.BlockSpec((tn // LANES, 1, RES, tm // RES, LANES),
                                      lambda i, j: (j, i // halves, 0, i % halves, 0)))
        scratch.append(pltpu.VMEM((d // LANES, tm, LANES), F32))
        scratch.append(pltpu.VMEM((d // LANES, tm // 4, LANES), F32))

        zh, lb_logits, gnorm, batch, seq, ts = hgrn
        nh = HGRN_HEADS_PER_STEP
        pairs, seq_tiles = HGRN_HEADS // nh, seq // ts
        assert grid[0] * grid[1] == batch * pairs * seq_tiles

        def where(i, j):
            step = i * n_col_tiles + j
            return (step // seq_tiles) % pairs, (step // (pairs * seq_tiles)) * seq_tiles + step % seq_tiles

        def zspec(cb):
            return pl.BlockSpec((nh, ts, LANES), lambda i, j: (cb // nh + where(i, j)[0], where(i, j)[1], 0))

        extra_in = [zh, zh, zh, zh, lb_logits, gnorm]
        extra_specs = [zspec(CB_HQ), zspec(CB_HF), zspec(CB_HI), zspec(CB_HG),
                       pl.BlockSpec((lb_logits.shape[0], nh * LANES), lambda i, j: (0, where(i, j)[0])),
                       pl.BlockSpec((1, LANES), lambda i, j: (0, 0))]
        out_shapes.append(jax.ShapeDtypeStruct((HGRN_HEADS, batch * seq, LANES), BF16))
        out_specs.append(pl.BlockSpec((nh, ts, LANES), lambda i, j: (where(i, j)[0], where(i, j)[1], 0)))
        scratch.append(pltpu.VMEM((nh, HGRN_HEAD, HGRN_HEAD), F32))
        steps_per_seq = seq_tiles
    else:
        out_shapes.append(jax.ShapeDtypeStruct((n // LANES, m, LANES), BF16))
        out_specs.append(pl.BlockSpec((tn // LANES, tm, LANES), lambda i, j: (j, i, 0)))
        steps_per_seq = None
    outs = pl.pallas_call(
        functools.partial(_inproj_kernel, by_residue=by_residue,
                          q_cols=D_ATTN if by_residue else 0, q_scale=ATTN_HEAD ** -0.5 * LOG2_E,
                          hgrn_steps_per_seq=steps_per_seq),
        grid=grid,
        in_specs=[
            pl.BlockSpec((tm, d), lambda i, j: (i, 0)),
            pl.BlockSpec((1, d), lambda i, j: (0, 0)),
            pl.BlockSpec((pl.Element(d), pl.Element(tn)), lambda i, j: (0, pl.multiple_of(col_start(j), 2 * LANES))),
        ] + extra_specs,
        out_specs=out_specs,
        out_shape=out_shapes,
        scratch_shapes=scratch,
        compiler_params=pltpu.CompilerParams(
            dimension_semantics=("arbitrary", "arbitrary") if by_residue else ("parallel", "arbitrary"),
            vmem_limit_bytes=VMEM_LIMIT),
        name="inproj_attn_hgrn2" if by_residue else "inproj_token",
    )(x2, gain, w_bf16, *extra_in)
    return outs if by_residue else outs[0]


def _hgrn_phases(q_ref, f_ref, i_ref, g_ref, lbl_ref, gn_ref, o_ref, st_ref, first_of_sequence):
    nh, ts, _ = q_ref.shape
    heads = range(nh)
    c_len = HGRN_CHUNK
    chunks = [slice(c * c_len, (c + 1) * c_len) for c in range(ts // c_len)]
    row = lax.broadcasted_iota(jnp.int32, (c_len, c_len), 0)
    col = lax.broadcasted_iota(jnp.int32, (c_len, c_len), 1)
    causal = col <= row
    tril = causal.astype(BF16)

    lbl = lbl_ref[...]
    e = jnp.exp(lbl - jnp.max(lbl, axis=0, keepdims=True))
    lb_all = e[0:1, :] / jnp.sum(e, axis=0, keepdims=True)
    lb = [lb_all[:, h * LANES:(h + 1) * LANES] for h in heads]

    q = [_silu(q_ref[h].astype(F32)) for h in heads]
    half_range = [0.5 * (1.0 - lb[h]) for h in heads]
    mid = [lb[h] + half_range[h] for h in heads]
    f = [mid[h] + half_range[h] * jnp.tanh(0.5 * f_ref[h].astype(F32)) for h in heads]
    k = [1.0 - f[h] for h in heads]
    log_f = [jnp.log2(f[h]) for h in heads]
    hilo = []
    for h in heads:
        hi = log_f[h].astype(BF16)
        lo = (log_f[h] - hi.astype(F32)).astype(BF16)
        hilo.append(jnp.concatenate([hi, lo], axis=-1))
    yield
    cs = [jnp.concatenate([jnp.dot(tril, hilo[h][sl], preferred_element_type=F32) for sl in chunks], axis=0)
          for h in heads]
    yield
    b_cum =[cs[h][:, :LANES] + cs[h][:, LANES:] for h in heads]
    decay = [[jnp.exp2(b_cum[h][sl.stop - 1:sl.stop, :]) for sl in chunks] for h in heads]
    q_dec = [(q[h] * jnp.exp2(b_cum[h])).astype(BF16) for h in heads]
    k_dec_f = [k[h] * jnp.exp2(-b_cum[h]) for h in heads]
    k_dec = [k_dec_f[h].astype(BF16) for h in heads]
    k_end = [jnp.concatenate([k_dec_f[h][sl] * decay[h][c] for c, sl in enumerate(chunks)], axis=0).astype(BF16)
             for h in heads]
    v = [i_ref[h] for h in heads]
    yield
    kv_t = [[None] * len(chunks) for _ in heads]
    a = [[None] * len(chunks) for _ in heads]
    for c, sl in enumerate(chunks):
        for h in heads:
            kv_t[h][c] = lax.dot_general(v[h][sl], k_end[h][sl], TN_DIMS, preferred_element_type=F32)
            a[h][c] = lax.dot_general(q_dec[h][sl], k_dec[h][sl], NT_DIMS, preferred_element_type=F32)
    yield
    a = [[jnp.where(causal, a_c, 0.0).astype(BF16) for a_c in a[h]] for h in heads]
    entering = [[] for _ in heads]
    for h in heads:
        st = jnp.where(first_of_sequence, 0.0, st_ref[h])
        for c in range(len(chunks)):
            entering[h].append(st.astype(BF16))
            st = st * decay[h][c] + kv_t[h][c]
        st_ref[h] = st
    yield
    o = [jnp.concatenate(
        [jnp.dot(a[h][c], v[h][sl], preferred_element_type=F32)
         + lax.dot_general(q_dec[h][sl], entering[h][c], NT_DIMS, preferred_element_type=F32)
         for c, sl in enumerate(chunks)], axis=0) for h in heads]
    yield
    gn = gn_ref[...]
    for h in heads:
        o_h = o[h] * lax.rsqrt(jnp.mean(o[h] * o[h], axis=-1, keepdims=True) + NORM_EPS) * gn
        o_ref[h] = (o_h * _silu(g_ref[h].astype(F32))).astype(BF16)


def _unit_positions(p):
    rho = lax.broadcasted_iota(jnp.int32, (BAND, 2 * BAND), 0)
    kap = lax.broadcasted_iota(jnp.int32, (BAND, 2 * BAND), 1)
    if p == 2:
        return BAND + rho, kap
    if p == 1:
        q_pos = BAND + 4 * (rho & 31) + (rho >> 5)
        k_pos = (kap >> 7) * BAND + 4 * (kap & 31) + ((kap >> 5) & 3)
        return q_pos, k_pos
    q_pos = BAND + RES * (rho & 7) + (rho >> 3)
    k_pos = RES * (kap & 15) + (kap >> 4)
    return q_pos, k_pos


def _scores(q2, kb):
    return lax.dot_general(q2, kb, NT_DIMS, preferred_element_type=F32)


def _probs(s, bias0, bias1, head0):
    s0 = s[:BAND] + bias0
    m0 = jnp.max(s0, axis=-1, keepdims=True)
    p0 = jnp.exp2(s0 - m0).astype(BF16)
    s1 = s[BAND:] + bias1
    m1 = jnp.max(s1, axis=-1, keepdims=True)
    p1 = jnp.exp2(s1 - m1).astype(BF16)
    return jnp.concatenate([p0, p1], axis=1), jnp.where(head0, m0, m1)


def _attn_kernel(slopes_ref, q_ref, k_ref, v_ref, kp_ref, g_ref, o_ref,
                 bias_ref, qm_ref, vm_ref, qf_ref, kf_ref, vf_ref, acc_ref, mx_ref, den_ref, nat_ref):
    hp = pl.program_id(1)
    tile = pl.program_id(2)
    first_tile = jnp.where(tile == 0, 1, 0)
    slot = lax.bitwise_and(tile, 1)
    lane = lax.broadcasted_iota(jnp.int32, (1, LANES), 1)
    head0 = lane < ATTN_HEAD
    prev_rows = SUBLANES

    @pl.when(tile == 0)
    def _():
        vm_ref[0, 1] = jnp.zeros((RES, BAND, LANES), BF16)
        vm_ref[1, 1] = jnp.zeros((RES, BAND, LANES), BF16)
        for p, (_, dil) in enumerate(DILATED_PATTERNS):
            q_pos, k_pos = _unit_positions(p)
            delta = q_pos - k_pos
            in_band = (delta >= 0) & (delta <= BAND)
            not_prev = k_pos >= BAND
            delta_f = delta.astype(F32)
            for hh in range(2):
                slope = slopes_ref[2 * hp + hh]
                bias = -(slope * (float(dil) * LOG2_E)) * delta_f
                bias_ref[p, hh, 0] = jnp.where(in_band, bias, MASK_VALUE)
                bias_ref[p, hh, 1] = jnp.where(in_band & not_prev, bias, MASK_VALUE)

    def split_heads(x):
        zero = jnp.zeros_like(x)
        return jnp.where(head0, x, zero), jnp.where(head0, zero, x)

    qm_ref[0], qm_ref[1] = split_heads(q_ref[0, 0])
    vm_ref[0, slot], vm_ref[1, slot] = split_heads(v_ref[0, 0])

    def unpack_for_d1():
        qf_ref[...] = q_ref[0, 0].astype(F32)
        packed_rows = 2 * SUBLANES
        v_tail = (vm_ref[0, 1 - slot, :, BAND - packed_rows:, :].astype(F32)
                  + vm_ref[1, 1 - slot, :, BAND - packed_rows:, :].astype(F32))
        vf_ref[:, :prev_rows, :] = v_tail[:, packed_rows - prev_rows:, :]
        vf_ref[:, prev_rows:, :] = v_ref[0, 0].astype(F32)
        k_tail = kp_ref[0, 0, :, BAND - packed_rows:, :].astype(F32)
        kf_ref[:, :prev_rows, :] = k_tail[:, packed_rows - prev_rows:, :]
        kf_ref[:, prev_rows:, :] = k_ref[0, 0].astype(F32)

    key_row = lax.broadcasted_iota(jnp.int32, (4 * BAND, LANES), 0)
    key_lane = lax.broadcasted_iota(jnp.int32, (4 * BAND, LANES), 1)
    den_cols = ((key_row < 2 * BAND) == (key_lane < ATTN_HEAD)).astype(BF16)

    def pv(pc, v2):
        return jnp.dot(pc, jnp.concatenate([v2, den_cols], axis=1), preferred_element_type=F32)

    rows4 = BAND // 4

    def unit_d16(r):
        def q2():
            return jnp.concatenate([qm_ref[0, r], qm_ref[1, r]], axis=0)

        def kb():
            return jnp.concatenate([kp_ref[0, 0, r], k_ref[0, 0, r]], axis=0)

        def v2():
            return jnp.concatenate([vm_ref[0, 1 - slot, r], vm_ref[0, slot, r],
                                    vm_ref[1, 1 - slot, r], vm_ref[1, slot, r]], axis=0)

        def store(acc, den, m_full):
            acc_ref[1, r] = acc
            den_ref[1, r] = den
            mx_ref[1, r] = m_full
        return 2, first_tile, q2, kb, v2, store

    def unit_d4(c, n):
        cur = slice(n * rows4, (n + 1) * rows4)
        prv = slice((n - 1) * rows4, n * rows4) if n else slice(BAND - rows4, BAND)
        slabs = [4 * a + c for a in range(4)]

        def q2():
            return jnp.concatenate([qm_ref[hh, sb, cur] for hh in range(2) for sb in slabs], axis=0)

        def kb():
            src = k_ref if n else kp_ref
            return jnp.concatenate([src[0, 0, sb, prv] for sb in slabs] + [k_ref[0, 0, sb, cur] for sb in slabs], axis=0)

        def v2():
            prev_slot = slot if n else 1 - slot
            return jnp.concatenate(
                [vm_ref[hh, prev_slot, sb, prv] if part == 0 else vm_ref[hh, slot, sb, cur]
                 for hh in range(2) for part in range(2) for sb in slabs], axis=0)

        def store(acc, den, m_full):
            for a, sb in enumerate(slabs):
                piece = slice(a * rows4, (a + 1) * rows4)
                acc_ref[0, sb, cur] = acc[piece]
                den_ref[0, sb, cur] = den[piece]
                mx_ref[0, sb, cur] = m_full[piece]
        return 1, (first_tile if n == 0 else 0), q2, kb, v2, store

    def unit_d1(n):
        q_rows = slice(n * SUBLANES, (n + 1) * SUBLANES)
        k_rows = slice(n * SUBLANES, (n + 2) * SUBLANES)
        tok = n * BAND

        def q2():
            return jnp.concatenate(split_heads(qf_ref[:, q_rows, :].reshape(BAND, LANES)), axis=0).astype(BF16)

        def kb():
            return kf_ref[:, k_rows, :].reshape(2 * BAND, LANES).astype(BF16)

        def v2():
            return jnp.concatenate(split_heads(vf_ref[:, k_rows, :].reshape(2 * BAND, LANES)), axis=0).astype(BF16)

        def store(acc, den, m_full):
            others = [tuple(ref[p, :, q_rows, :].reshape(BAND, LANES) for ref in (acc_ref, den_ref, mx_ref))
                      for p in range(2)]
            m_all = jnp.maximum(jnp.maximum(m_full, others[0][2]), others[1][2])
            w = jnp.exp2(m_full - m_all)
            num = w * acc
            total = w * den
            for acc_p, den_p, m_p in others:
                w = jnp.exp2(m_p - m_all)
                num = num + w * acc_p
                total = total + w * den_p
            y = (num / total).reshape(RES, SUBLANES, LANES)
            for r in range(RES):
                nat_ref[pl.ds(tok + r, SUBLANES, stride=RES), :] = y[r]
            rows = slice(tok, tok + BAND)
            gate = _silu(g_ref[0, rows, :].astype(F32))
            o_ref[0, rows, :] = (nat_ref[rows, :] * gate).astype(BF16)
        return 0, (first_tile if n == 0 else 0), q2, kb, v2, store

    units = [unit_d16(r) for r in range(RES)]
    d1_ready = []
    for n4 in range(4):
        for c in range(4):
            units.append(unit_d4(c, n4))
            if d1_ready:
                units.append(unit_d1(d1_ready.pop(0)))
        d1_ready.extend(range(4 * n4, 4 * n4 + 4))
    units.extend(unit_d1(n) for n in d1_ready)

    s, pm = {}, {}
    sm_lag, pv_lag = 1, 3
    for i in range(len(units) + pv_lag):
        if i == 4:
            unpack_for_d1()
        if i < len(units):
            s[i] = _scores(units[i][2](), units[i][3]())
        if 0 <= i - sm_lag < len(units):
            p, var = units[i - sm_lag][:2]
            pm[i - sm_lag] = _probs(s.pop(i - sm_lag), bias_ref[p, 0, var], bias_ref[p, 1, var], head0)
        if 0 <= i - pv_lag < len(units):
            pc, m_full = pm.pop(i - pv_lag)
            od = pv(pc, units[i - pv_lag][4]())
            units[i - pv_lag][5](od[:, :LANES], od[:, LANES:], m_full)


def _attn(za, zh, slopes, *, batch, seq):
    nt_ = seq // ATTN_TILE
    n_pat = len(DILATED_PATTERNS)
    blk = (1, 1, RES, BAND, LANES)

    def zspec(cb):
        return pl.BlockSpec(blk, lambda b, h, t, sl: (cb + h, b * nt_ + t, 0, 0, 0))

    prev_keys = pl.BlockSpec(blk, lambda b, h, t, sl: (CB_AK + h, b * nt_ + jnp.maximum(t - 1, 0), 0, 0, 0))
    gate = pl.BlockSpec((1, ATTN_TILE, LANES), lambda b, h, t, sl: (CB_AG + h, b * nt_ + t, 0))
    grid_spec = pltpu.PrefetchScalarGridSpec(
        num_scalar_prefetch=1,
        grid=(batch, HEAD_PAIRS, nt_),
        in_specs=[zspec(CB_AQ), zspec(CB_AK), zspec(CB_AV), prev_keys, gate],
        out_specs=pl.BlockSpec((1, ATTN_TILE, LANES), lambda b, h, t, sl: (h, b * nt_ + t, 0)),
        scratch_shapes=[
            pltpu.VMEM((n_pat, 2, 2, BAND, 2 * BAND), F32),
            pltpu.VMEM((2, RES, BAND, LANES), BF16),
            pltpu.VMEM((2, 2, RES, BAND, LANES), BF16),
            pltpu.VMEM((RES, BAND, LANES), F32),
            pltpu.VMEM((RES, SUBLANES + BAND, LANES), F32),
            pltpu.VMEM((RES, SUBLANES + BAND, LANES), F32),
            pltpu.VMEM((n_pat - 1, RES, BAND, LANES), F32),
            pltpu.VMEM((n_pat - 1, RES, BAND, LANES), F32),
            pltpu.VMEM((n_pat - 1, RES, BAND, LANES), F32),
            pltpu.VMEM((ATTN_TILE, LANES), F32),
        ],
    )
    return pl.pallas_call(
        _attn_kernel,
        grid_spec=grid_spec,
        out_shape=jax.ShapeDtypeStruct((HEAD_PAIRS, batch * seq, LANES), BF16),
        compiler_params=pltpu.CompilerParams(
            dimension_semantics=("parallel", "parallel", "arbitrary"), vmem_limit_bytes=VMEM_LIMIT),
        name="dilated_attn",
    )(slopes, za, za, za, za, zh)


def _outproj_kernel(yh_ref, ya_ref, x_ref, w_ref, g_ref, o_ref):
    y = jnp.concatenate([yh_ref[c] for c in range(yh_ref.shape[0])]
                        + [ya_ref[c] for c in range(ya_ref.shape[0])], axis=-1)
    r = x_ref[...] + jnp.dot(y, w_ref[...], preferred_element_type=F32)
    ms = jnp.mean(r * r, axis=-1, keepdims=True)
    o_ref[...] = r * lax.rsqrt(ms + NORM_EPS) * g_ref[...]


def _outproj(yh3, ya3, x2, w_bf16, gain, *, tm):
    m, d = x2.shape
    return pl.pallas_call(
        _outproj_kernel,
        grid=(m // tm,),
        in_specs=[
            pl.BlockSpec((yh3.shape[0], tm, LANES), lambda i: (0, i, 0)),
            pl.BlockSpec((ya3.shape[0], tm, LANES), lambda i: (0, i, 0)),
            pl.BlockSpec((tm, d), lambda i: (i, 0)),
            pl.BlockSpec(w_bf16.shape, lambda i: (0, 0)),
            pl.BlockSpec((1, d), lambda i: (0, 0)),
        ],
        out_specs=pl.BlockSpec((tm, d), lambda i: (i, 0)),
        out_shape=jax.ShapeDtypeStruct((m, d), F32),
        compiler_params=pltpu.CompilerParams(
            dimension_semantics=("parallel",), vmem_limit_bytes=VMEM_LIMIT),
        name="outproj",
    )(yh3, ya3, x2, w_bf16, gain)


def kernel(x, norm_gain, w_in, lb_logits, hgrn_gnorm, w_out, final_gain):
    batch, seq, d = x.shape
    assert d == D_MODEL and w_in.shape == (1, D_MODEL, 4 * D_HGRN + 4 * D_ATTN) and lb_logits.shape[0] == 2
    m = batch * seq
    assert seq % ATTN_TILE == 0 and seq % HGRN_TILE == 0 and m % PROJ_ROWS == 0 and m % OUT_ROWS == 0
    assert D_HGRN == D_ATTN == TOKEN_PROJ_COLS

    x2 = x.reshape(m, d)
    w_in_bf16 = w_in[0].astype(BF16)
    gain = norm_gain[0:1]
    zh = _inproj(x2, gain, w_in_bf16, tm=PROJ_ROWS, tn=TOKEN_PROJ_COLS, n_col_tiles=5,
                 col_start=lambda j: jnp.where(j < 4, j, 7) * TOKEN_PROJ_COLS, by_residue=False)
    za, yh3 = _inproj(x2, gain, w_in_bf16, tm=PROJ_ROWS, tn=RESIDUE_PROJ_COLS,
                      n_col_tiles=3 * D_ATTN // RESIDUE_PROJ_COLS,
                      col_start=lambda j: 4 * D_HGRN + j * RESIDUE_PROJ_COLS, by_residue=True,
                      hgrn=(zh, lb_logits, hgrn_gnorm[0:1], batch, seq, HGRN_TILE))
    slopes = jnp.exp2(-8.0 * jnp.arange(1, ATTN_HEADS + 1, dtype=F32) / ATTN_HEADS)
    ya3 = _attn(za, zh, slopes, batch=batch, seq=seq)
    out = _outproj(yh3, ya3, x2, w_out[0].astype(BF16), final_gain.reshape(1, d), tm=OUT_ROWS)
    return out.reshape(batch, seq, d)
```
